```python
import jax, jax.numpy as jnp
from jax import lax
import numpy as np

D_MODEL = 1024
BATCH = 4
SEQ = 4096
DEPTH = 2
DEC_BATCH = 32
DEC_SEQ = 4
PAST_LEN = 16384
PAGE_SIZE = 128

N_META = 16
BLOCK = 128
META_PAD = BLOCK - N_META
CHUNK_A = 128
CHUNK_C = 64
H_A = 4
DK_A = 128
DV_A = 128
H_B = 8
DH_B = 64
H_C = 4
DK_C = 128
DV_C = 256
GATE_RANK = 16
GATE_TAU = 16.0
SB_BIAS_INIT = -6.5
D_FF = 4 * D_MODEL
N_EVEN = (DEPTH + 1) // 2
N_ODD = DEPTH // 2
MIX_EVEN = H_A * DV_A + H_B * DH_B
MIX_ODD = H_C * DV_C
EVEN_SPLITS = (H_A * DK_A, H_A * DK_A, H_A * DV_A, H_A * DV_A, H_A, H_A, H_B * DH_B, H_B * DH_B, H_B * DH_B)
ODD_SPLITS = (H_C * DK_C, H_C * DK_C, H_C * DV_C, H_C * DV_C, GATE_RANK)
EVEN_IN = sum(EVEN_SPLITS)
ODD_IN = sum(ODD_SPLITS)
EPS = 1e-6

kernel_name = "hybrid_mlstm_stickbreak_gla_decode_step"

f32 = jnp.float32


def _rms(x, g):
    xf = x.astype(f32)
    y = xf * lax.rsqrt(jnp.mean(xf * xf, axis=-1, keepdims=True) + EPS)
    return (y * g.astype(f32)).astype(x.dtype)


def _head_rms(h, g, n_heads):
    shp = h.shape
    hf = h.astype(f32).reshape(shp[:-1] + (n_heads, shp[-1] // n_heads))
    hf = hf * lax.rsqrt(jnp.mean(hf * hf, axis=-1, keepdims=True) + EPS)
    return hf.reshape(shp) * g.astype(f32)


def _split(x, sizes):
    out, off = [], 0
    for s in sizes:
        out.append(x[..., off:off + s])
        off += s
    return out


def _heads(t, h):
    return t.reshape(t.shape[:-1] + (h, t.shape[-1] // h))


def _to_chunks(x, c):
    B, L, H = x.shape[:3]
    rest = x.shape[3:]
    x = x.reshape((B, L // c, c, H) + rest)
    return x.transpose((1, 0, 3, 2) + tuple(range(4, x.ndim)))


def _from_chunks(y):
    nc, B, H, c, d = y.shape
    return y.transpose(1, 0, 3, 2, 4).reshape(B, nc * c, H, d)


def _mlstm_chunked(q, k, v, log_i, log_f, C0, n0, m0, chunk):
    xs = tuple(_to_chunks(a.astype(f32), chunk) for a in (q, k, v, log_i, log_f))
    causal = jnp.tril(jnp.ones((chunk, chunk), bool))

    def step(carry, inp):
        C, n, m = carry
        qc, kc, vc, ic, fc = inp
        b = jnp.cumsum(fc, axis=-1)
        d_log = jnp.where(causal, b[..., :, None] - b[..., None, :] + ic[..., None, :], -jnp.inf)
        inter = b + m[..., None]
        m_t = jnp.maximum(inter, jnp.max(d_log, axis=-1))
        s = jnp.einsum('bhtk,bhsk->bhts', qc, kc) * jnp.exp(d_log - m_t[..., None])
        w_inter = jnp.exp(inter - m_t)
        num = w_inter[..., None] * jnp.einsum('bhtk,bhkv->bhtv', qc, C) + jnp.einsum('bhts,bhsv->bhtv', s, vc)
        den = w_inter * jnp.einsum('bhtk,bhk->bht', qc, n) + jnp.sum(s, axis=-1)
        h = num / jnp.maximum(jnp.abs(den), jnp.exp(-m_t))[..., None]
        b_last = b[..., -1]
        g = b_last[..., None] - b + ic
        m_new = jnp.maximum(b_last + m, jnp.max(g, axis=-1))
        w_k = jnp.exp(g - m_new[..., None])
        w_d = jnp.exp(b_last + m - m_new)
        C_new = w_d[..., None, None] * C + jnp.einsum('bhs,bhsk,bhsv->bhkv', w_k, kc, vc)
        n_new = w_d[..., None] * n + jnp.einsum('bhs,bhsk->bhk', w_k, kc)
        return (C_new, n_new, m_new), h

    (C, n, m), hs = lax.scan(step, (C0.astype(f32), n0.astype(f32), m0.astype(f32)), xs)
    return _from_chunks(hs), C, n, m


def _gla_chunked(q, k, v, log_g, S0, chunk):
    xs = tuple(_to_chunks(a.astype(f32), chunk) for a in (q, k, v, log_g))
    causal = jnp.tril(jnp.ones((chunk, chunk), bool))

    def step(S, inp):
        qc, kc, vc, gc = inp
        b = jnp.cumsum(gc, axis=2)
        o_inter = jnp.einsum('bhtk,bhkv->bhtv', qc * jnp.exp(b), S)
        rel = jnp.where(causal[:, :, None], b[:, :, :, None, :] - b[:, :, None, :, :], -jnp.inf)
        a = jnp.einsum('bhtk,bhsk,bhtsk->bhts', qc, kc, jnp.exp(rel))
        o = o_inter + jnp.einsum('bhts,bhsv->bhtv', a, vc)
        b_last = b[:, :, -1]
        S_new = jnp.exp(b_last)[..., None] * S + jnp.einsum('bhsk,bhsv->bhkv', kc * jnp.exp(b_last[:, :, None, :] - b), vc)
        return S_new, o

    S, os_ = lax.scan(step, S0.astype(f32), xs)
    return _from_chunks(os_), S


def _stick_breaking(q, k, v, mask, bias):
    z = jnp.einsum('bthd,bshd->bhts', q.astype(f32), k.astype(f32)) * (DH_B ** -0.5) + bias.astype(f32)[None, :, None, None]
    log_stay = jnp.where(mask, jax.nn.log_sigmoid(-z), 0.0)
    after = lax.cumsum(log_stay, axis=3, reverse=True)
    after = jnp.concatenate([after[..., 1:], jnp.zeros_like(after[..., :1])], axis=-1)
    w = jnp.where(mask, jnp.exp(jax.nn.log_sigmoid(z) + after), 0.0)
    return jnp.einsum('bhts,bshd->bthd', w, v.astype(f32))


def _sb_prompt(q, k, v, valid, bias):
    B, L, H, d = q.shape
    nb = L // BLOCK
    qb = q.reshape(B, nb, BLOCK, H, d).transpose(1, 0, 2, 3, 4)
    pos = jnp.arange(L)

    def one(args):
        q_blk, blk = args
        t_pos = blk * BLOCK + jnp.arange(BLOCK)
        mask = (pos[None, :] < t_pos[:, None]) & valid[None, :]
        return _stick_breaking(q_blk, k, v, mask, bias)

    out = lax.map(one, (qb, jnp.arange(nb)))
    return out.transpose(1, 0, 2, 3, 4).reshape(B, L, H, d)


def _even_proj(xn, w_in, b_gate):
    q_a, k_a, v_a, o_a, i_a, f_a, q_b, k_b, v_b = _split(xn @ w_in, EVEN_SPLITS)
    log_i = i_a.astype(f32) + b_gate[:H_A].astype(f32)
    log_f = jax.nn.log_sigmoid(f_a.astype(f32) + b_gate[H_A:].astype(f32))
    return (_heads(q_a, H_A), _heads(k_a, H_A) * (DK_A ** -0.5), _heads(v_a, H_A), o_a, log_i, log_f,
            _heads(q_b, H_B), _heads(k_b, H_B), _heads(v_b, H_B))


def _even_out(h_a, o_a, h_b, g_head, w_out, dt):
    B, L = h_a.shape[:2]
    y_a = _head_rms(h_a.reshape(B, L, -1), g_head, H_A) * jax.nn.sigmoid(o_a.astype(f32))
    y = jnp.concatenate([y_a, h_b.reshape(B, L, -1)], axis=-1).astype(dt)
    return y @ w_out


def _odd_proj(xn, w_in, w_gate_up, b_gate_up):
    q, k, v, r, g_down = _split(xn @ w_in, ODD_SPLITS)
    log_g = jax.nn.log_sigmoid((g_down @ w_gate_up).astype(f32) + b_gate_up.astype(f32)) / GATE_TAU
    return _heads(q, H_C) * (DK_C ** -0.5), _heads(k, H_C), _heads(v, H_C), r, _heads(log_g, H_C)


def _odd_out(o, r, g_head, w_out, dt):
    B, L = o.shape[:2]
    y = _head_rms(o.reshape(B, L, -1), g_head, H_C) * jax.nn.silu(r.astype(f32))
    return y.astype(dt) @ w_out


def _mlp(x, w_up, w_down):
    h = jax.nn.relu(x @ w_up)
    return (h * h) @ w_down


def setup_inputs(seed: int = 0) -> dict:
    key = jax.random.key(seed)
    ks = jax.random.split(key, 24)
    n_pages = PAST_LEN // PAGE_SIZE
    n_pool = (DEC_BATCH * n_pages * 5) // 4
    nrm = jax.random.normal
    page_table = jax.random.permutation(ks[8], n_pool)[:DEC_BATCH * n_pages].reshape(DEC_BATCH, n_pages).astype(jnp.int32)
    b_gate_even = jnp.concatenate([0.1 * nrm(ks[12], (N_EVEN, H_A)), 3.0 + 0.1 * nrm(ks[13], (N_EVEN, H_A))], axis=-1)
    return {
        "x_prompt": nrm(ks[0], (BATCH, SEQ, D_MODEL), f32),
        "x_sample": nrm(ks[1], (DEC_BATCH, DEC_SEQ, D_MODEL), f32),
        "cache_sb_k": nrm(ks[2], (N_EVEN, n_pool, PAGE_SIZE, H_B, DH_B), f32),
        "cache_sb_v": nrm(ks[3], (N_EVEN, n_pool, PAGE_SIZE, H_B, DH_B), f32),
        "state_mlstm_c": nrm(ks[4], (N_EVEN, DEC_BATCH, H_A, DK_A, DV_A), f32),
        "state_mlstm_n": nrm(ks[5], (N_EVEN, DEC_BATCH, H_A, DK_A), f32),
        "state_mlstm_m": nrm(ks[6], (N_EVEN, DEC_BATCH, H_A), f32),
        "state_gla_s": 0.1 * nrm(ks[7], (N_ODD, DEC_BATCH, H_C, DK_C, DV_C), f32),
        "page_table": page_table,
        "meta_tokens": nrm(ks[9], (N_META, D_MODEL), f32),
        "norm_gains": 1.0 + 0.02 * nrm(ks[10], (DEPTH, 4, D_MODEL), f32),
        "w_in_even": nrm(ks[11], (N_EVEN, D_MODEL, EVEN_IN), f32) * D_MODEL ** -0.5,
        "b_gate_even": b_gate_even,
        "b_sb_even": SB_BIAS_INIT + 0.3 * nrm(ks[23], (N_EVEN, H_B), f32),
        "g_head_even": 1.0 + 0.02 * nrm(ks[14], (N_EVEN, H_A * DV_A), f32),
        "w_out_even": nrm(ks[15], (N_EVEN, MIX_EVEN, D_MODEL), f32) * MIX_EVEN ** -0.5,
        "w_in_odd": nrm(ks[16], (N_ODD, D_MODEL, ODD_IN), f32) * D_MODEL ** -0.5,
        "w_gate_up_odd": nrm(ks[17], (N_ODD, GATE_RANK, H_C * DK_C), f32) * GATE_RANK ** -0.5,
        "b_gate_up_odd": 0.1 * nrm(ks[18], (N_ODD, H_C * DK_C), f32),
        "g_head_odd": 1.0 + 0.02 * nrm(ks[19], (N_ODD, H_C * DV_C), f32),
        "w_out_odd": nrm(ks[20], (N_ODD, MIX_ODD, D_MODEL), f32) * MIX_ODD ** -0.5,
        "w_up": nrm(ks[21], (DEPTH, D_MODEL, D_FF), f32) * D_MODEL ** -0.5,
        "w_down": nrm(ks[22], (DEPTH, D_FF, D_MODEL), f32) * D_FF ** -0.5,
    }


def reference(x_prompt, x_sample, cache_sb_k, cache_sb_v, state_mlstm_c, state_mlstm_n, state_mlstm_m, state_gla_s,
              page_table, meta_tokens, norm_gains, w_in_even, b_gate_even, b_sb_even, g_head_even, w_out_even,
              w_in_odd, w_gate_up_odd, b_gate_up_odd, g_head_odd, w_out_odd, w_up, w_down):
    dt = x_prompt.dtype
    B = x_prompt.shape[0]
    DB, T = x_sample.shape[:2]
    pad = jnp.zeros((B, META_PAD, D_MODEL), dt)
    meta = jnp.broadcast_to(meta_tokens.astype(dt)[None], (B, N_META, D_MODEL))
    hp = jnp.concatenate([pad, meta, x_prompt], axis=1)
    Lp = hp.shape[1]
    valid = jnp.arange(Lp) >= META_PAD
    hs = x_sample

    p_k, p_v, p_c, p_n, p_m, p_s = [], [], [], [], [], []
    s_k, s_v, s_c, s_n, s_m, s_s = [], [], [], [], [], []

    for layer in range(DEPTH):
        g = norm_gains[layer]
        xpn = _rms(hp, g[0])
        xsn = _rms(hs, g[0])
        if layer % 2 == 0:
            e = layer // 2
            qa, ka, va, oa, li, lf, qb, kb, vb = _even_proj(xpn, w_in_even[e], b_gate_even[e])
            li = jnp.where(valid[None, :, None], li, -jnp.inf)
            lf = jnp.where(valid[None, :, None], lf, 0.0)
            ha, C, n, m = _mlstm_chunked(qa, ka, va, li, lf,
                                         jnp.zeros((B, H_A, DK_A, DV_A), f32), jnp.zeros((B, H_A, DK_A), f32),
                                         jnp.zeros((B, H_A), f32), CHUNK_A)
            hb = _sb_prompt(qb, kb, vb, valid, b_sb_even[e])
            mix_p = _even_out(ha, oa, hb, g_head_even[e], w_out_even[e], dt)
            p_k.append(kb[:, META_PAD:])
            p_v.append(vb[:, META_PAD:])
            p_c.append(C)
            p_n.append(n)
            p_m.append(m)
            qa, ka, va, oa, li, lf, qb, kb, vb = _even_proj(xsn, w_in_even[e], b_gate_even[e])
            ha, C, n, m = _mlstm_chunked(qa, ka, va, li, lf, state_mlstm_c[e], state_mlstm_n[e], state_mlstm_m[e], T)
            k_past = cache_sb_k[e][page_table].reshape(DB, -1, H_B, DH_B)
            v_past = cache_sb_v[e][page_table].reshape(DB, -1, H_B, DH_B)
            P = k_past.shape[1]
            k_all = jnp.concatenate([k_past, kb.astype(k_past.dtype)], axis=1)
            v_all = jnp.concatenate([v_past, vb.astype(v_past.dtype)], axis=1)
            mask = jnp.arange(P + T)[None, :] < (P + jnp.arange(T))[:, None]
            hb = _stick_breaking(qb, k_all, v_all, mask, b_sb_even[e])
            mix_s = _even_out(ha, oa, hb, g_head_even[e], w_out_even[e], dt)
            s_k.append(kb)
            s_v.append(vb)
            s_c.append(C)
            s_n.append(n)
            s_m.append(m)
        else:
            o_i = layer // 2
            q, k, v, r, lg = _odd_proj(xpn, w_in_odd[o_i], w_gate_up_odd[o_i], b_gate_up_odd[o_i])
            k = jnp.where(valid[None, :, None, None], k, 0.0)
            lg = jnp.where(valid[None, :, None, None], lg, 0.0)
            o, S = _gla_chunked(q, k, v, lg, jnp.zeros((B, H_C, DK_C, DV_C), f32), CHUNK_C)
            mix_p = _odd_out(o, r, g_head_odd[o_i], w_out_odd[o_i], dt)
            p_s.append(S)
            q, k, v, r, lg = _odd_proj(xsn, w_in_odd[o_i], w_gate_up_odd[o_i], b_gate_up_odd[o_i])
            o, S = _gla_chunked(q, k, v, lg, state_gla_s[o_i], T)
            mix_s = _odd_out(o, r, g_head_odd[o_i], w_out_odd[o_i], dt)
            s_s.append(S)
        hp = hp + _rms(mix_p.astype(dt), g[1])
        hs = hs + _rms(mix_s.astype(dt), g[1])
        hp = hp + _rms(_mlp(_rms(hp, g[2]), w_up[layer], w_down[layer]).astype(dt), g[3])
        hs = hs + _rms(_mlp(_rms(hs, g[2]), w_up[layer], w_down[layer]).astype(dt), g[3])

    y_prompt = hp[:, META_PAD + N_META:]
    return (y_prompt, hs,
            jnp.stack(p_k), jnp.stack(p_v), jnp.stack(p_c), jnp.stack(p_n), jnp.stack(p_m), jnp.stack(p_s),
            jnp.stack(s_k), jnp.stack(s_v), jnp.stack(s_c), jnp.stack(s_n), jnp.stack(s_m), jnp.stack(s_s))
```

```python
import functools

import jax
import jax.numpy as jnp
import numpy as np
from jax import lax
from jax.experimental import pallas as pl
from jax.experimental.pallas import tpu as pltpu

f32 = jnp.float32
bf16 = jnp.bfloat16

BLOCK = 128
H_A, DK_A, DV_A = 4, 128, 128
H_B, DH_B = 8, 64
H_C, DK_C, DV_C = 4, 128, 256
GATE_RANK = 16
GATE_TAU = 16.0
CHUNK_A = 128
CHUNK_C = 64
EPS = 1e-6

LANES = 128
SUBLANES = 8
VMEM_LIMIT = 48 * 1024 * 1024

_NT = (((1,), (1,)), ((), ()))
_TN = (((0,), (0,)), ((), ()))


def _rms(x, g):
    return x * lax.rsqrt(jnp.mean(x * x, axis=-1, keepdims=True) + EPS) * g


def _softplus(z):
    return jnp.maximum(z, 0.0) + jnp.log1p(jnp.exp(-jnp.abs(z)))


def _log_sigmoid(z):
    return jnp.minimum(z, 0.0) - jnp.log1p(jnp.exp(-jnp.abs(z)))


def _split3(x):
    hi = x.astype(bf16)
    r = x - hi.astype(f32)
    mid = r.astype(bf16)
    lo = (r - mid.astype(f32)).astype(bf16)
    return hi, mid, lo


def _dot01_left(m01, x):
    return sum(jnp.dot(m01, p, preferred_element_type=f32) for p in _split3(x))


def _dot01_right(x, m01):
    return sum(jnp.dot(p, m01, preferred_element_type=f32) for p in _split3(x))


def _params(*sem):
    return pltpu.CompilerParams(dimension_semantics=sem, vmem_limit_bytes=VMEM_LIMIT)


def _in_proj_kernel(x_ref, g_ref, w_ref, *o_refs, outs):
    xn = _rms(x_ref[...], g_ref[...]).astype(bf16)
    cache = {}
    for o_ref, (off, n, scale) in zip(o_refs, outs):
        if (off, n) not in cache:
            cache[(off, n)] = jnp.dot(xn, w_ref[:, off:off + n], preferred_element_type=f32)
        y = cache[(off, n)]
        if scale != 1.0:
            y = y * scale
        o_ref[...] = y.astype(o_ref.dtype)


def _in_proj(x, g, w, outs, dtypes, tm):
    m, d = x.shape
    n_tot = w.shape[1]
    return pl.pallas_call(
        functools.partial(_in_proj_kernel, outs=outs),
        grid=(m // tm,),
        in_specs=[pl.BlockSpec((tm, d), lambda i: (i, 0)),
                  pl.BlockSpec((1, d), lambda i: (0, 0)),
                  pl.BlockSpec((d, n_tot), lambda i: (0, 0))],
        out_specs=[pl.BlockSpec((tm, n), lambda i: (i, 0)) for (_, n, _) in outs],
        out_shape=[jax.ShapeDtypeStruct((m, n), dt) for (_, n, _), dt in zip(outs, dtypes)],
        compiler_params=_params("parallel"),
        name="in_proj",
    )(x, g, w)


def _post_kernel(*refs, n_a, nf):
    a_refs = refs[:n_a]
    h_ref, wo_ref, g_ref, wu_ref, wd_ref, out_ref, h1_sc, xn_sc, acc_sc = refs[n_a:]
    f = pl.program_id(1)

    @pl.when(f == 0)
    def _():
        if n_a > 1:
            a = jnp.concatenate([r[...] for r in a_refs], axis=-1)
        else:
            a = a_refs[0][...]
        mix = jnp.dot(a, wo_ref[...], preferred_element_type=f32)
        h1 = h_ref[...] + _rms(mix, g_ref[1:2, :])
        h1_sc[...] = h1
        xn_sc[...] = _rms(h1, g_ref[2:3, :]).astype(bf16)
        acc_sc[...] = jnp.zeros_like(acc_sc)

    u = jnp.dot(xn_sc[...], wu_ref[...], preferred_element_type=f32)
    u = jnp.maximum(u, 0.0)
    acc_sc[...] += jnp.dot((u * u).astype(bf16), wd_ref[...], preferred_element_type=f32)

    @pl.when(f == nf - 1)
    def _():
        out_ref[...] = h1_sc[...] + _rms(acc_sc[...], g_ref[3:4, :])


def _post(a_list, h, w_out, gains, w_up, w_down, tm, tf):
    m, d = h.shape
    dff = w_up.shape[1]
    nf = dff // tf
    n_a = len(a_list)
    return pl.pallas_call(
        functools.partial(_post_kernel, n_a=n_a, nf=nf),
        grid=(m // tm, nf),
        in_specs=[pl.BlockSpec((tm, a.shape[1]), lambda i, f: (i, 0)) for a in a_list] + [
            pl.BlockSpec((tm, d), lambda i, f: (i, 0)),
            pl.BlockSpec(w_out.shape, lambda i, f: (0, 0)),
            pl.BlockSpec(gains.shape, lambda i, f: (0, 0)),
            pl.BlockSpec((d, tf), lambda i, f: (0, f)),
            pl.BlockSpec((tf, d), lambda i, f: (f, 0))],
        out_specs=pl.BlockSpec((tm, d), lambda i, f: (i, 0)),
        out_shape=jax.ShapeDtypeStruct((m, d), f32),
        scratch_shapes=[pltpu.VMEM((tm, d), f32), pltpu.VMEM((tm, d), bf16), pltpu.VMEM((tm, d), f32)],
        compiler_params=_params("parallel", "arbitrary"),
        name="post",
    )(*a_list, h, w_out, gains, w_up, w_down)


def _mlstm_kernel(q_ref, k_ref, v_ref, o_ref, g_ref, gt_ref, brow_ref, bcol_ref, c0_ref, n0_ref, m0_ref, gh_ref,
                  y_ref, C_ref, n_ref, m_ref, *, c, lo, hi):
    j = pl.program_id(1)

    @pl.when(j == 0)
    def _():
        C_ref[...] = c0_ref[...]
        n_ref[...] = n0_ref[...]
        m_ref[...] = m0_ref[...]

    row = lax.broadcasted_iota(jnp.int32, (c, c), 0)
    col = lax.broadcasted_iota(jnp.int32, (c, c), 1)
    causal = row >= col
    tril = jnp.where(causal, 1.0, 0.0).astype(bf16)
    triu = jnp.where(row <= col, 1.0, 0.0).astype(bf16)
    pos_c = j * c + lax.broadcasted_iota(jnp.int32, (c, 1), 0)
    valid_c = (pos_c >= lo) & (pos_c < hi)
    pos_r = j * c + lax.broadcasted_iota(jnp.int32, (1, c), 1)
    valid_r = (pos_r >= lo) & (pos_r < hi)

    gc = g_ref[0] + brow_ref[...]
    li_c = jnp.where(valid_c, gc, -jnp.inf)
    lf_c = jnp.where(valid_c, _log_sigmoid(gc), 0.0)
    b_c = _dot01_left(tril, lf_c)
    gr = gt_ref[0] + bcol_ref[...]
    li_r = jnp.where(valid_r, gr, -jnp.inf)
    lf_r = jnp.where(valid_r, _log_sigmoid(gr), 0.0)
    b_r = _dot01_right(lf_r, triu)

    for h in range(H_A):
        sl = slice(h * DK_A, (h + 1) * DK_A)
        q = q_ref[0, :, sl]
        k = k_ref[0, :, sl] * (DK_A ** -0.5)
        qb, kb, vb = q.astype(bf16), k.astype(bf16), v_ref[0, :, sl].astype(bf16)
        bc = b_c[:, H_A + h:H_A + h + 1]
        ic = li_c[:, h:h + 1]
        br = b_r[H_A + h:H_A + h + 1, :]
        ir = li_r[h:h + 1, :]
        m = m_ref[0, h:h + 1, :]
        C = C_ref[0, h]
        n = n_ref[0, h:h + 1, :]

        d_log = jnp.where(causal, bc - br + ir, -jnp.inf)
        inter = bc + m
        m_t = jnp.maximum(inter, jnp.max(d_log, axis=1, keepdims=True))
        s = lax.dot_general(qb, kb, _NT, preferred_element_type=f32) * jnp.exp(d_log - m_t)
        w_i = jnp.exp(inter - m_t)
        num = (w_i * jnp.dot(qb, C.astype(bf16), preferred_element_type=f32)
               + jnp.dot(s.astype(bf16), vb, preferred_element_type=f32))
        den = w_i * jnp.sum(q * n, axis=1, keepdims=True) + jnp.sum(s, axis=1, keepdims=True)
        hh = num / jnp.maximum(jnp.abs(den), jnp.exp(-m_t))

        b_last = bc[c - 1:c, :]
        gcol = b_last - bc + ic
        m_new = jnp.maximum(b_last + m, jnp.max(gcol, axis=0, keepdims=True))
        w_d = jnp.exp(b_last + m - m_new)
        kw = k * jnp.exp(gcol - m_new)
        C_ref[0, h] = w_d * C + lax.dot_general(kw.astype(bf16), vb, _TN, preferred_element_type=f32)
        n_ref[0, h:h + 1, :] = w_d * n + jnp.sum(kw, axis=0, keepdims=True)
        m_ref[0, h:h + 1, :] = m_new

        y = hh * lax.rsqrt(jnp.mean(hh * hh, axis=1, keepdims=True) + EPS) * gh_ref[:, sl]
        y_ref[0, :, sl] = (y * jax.nn.sigmoid(o_ref[0, :, sl])).astype(y_ref.dtype)


def _mlstm(q, k, v, o, g, gt, brow, bcol, c0, n0, m0, gh, lo, hi):
    nb, length, _ = q.shape
    c = CHUNK_A
    wide = pl.BlockSpec((1, c, H_A * DK_A), lambda b, j: (b, j, 0))
    st = lambda shape: pl.BlockSpec((1,) + shape, lambda b, j: (b,) + (0,) * len(shape))
    const = lambda a: pl.BlockSpec(a.shape, lambda b, j: (0,) * a.ndim)
    return pl.pallas_call(
        functools.partial(_mlstm_kernel, c=c, lo=lo, hi=hi),
        grid=(nb, length // c),
        in_specs=[wide, wide, wide, wide,
                  pl.BlockSpec((1, c, LANES), lambda b, j: (b, j, 0)),
                  pl.BlockSpec((1, 2 * H_A, c), lambda b, j: (b, 0, j)),
                  const(brow), const(bcol),
                  st((H_A, DK_A, DV_A)), st((H_A, DK_A)), st((H_A, 1)), const(gh)],
        out_specs=[wide, st((H_A, DK_A, DV_A)), st((H_A, DK_A)), st((H_A, 1))],
        out_shape=[jax.ShapeDtypeStruct((nb, length, H_A * DV_A), bf16),
                   jax.ShapeDtypeStruct((nb, H_A, DK_A, DV_A), f32),
                   jax.ShapeDtypeStruct((nb, H_A, DK_A), f32),
                   jax.ShapeDtypeStruct((nb, H_A, 1), f32)],
        compiler_params=_params("parallel", "arbitrary"),
        name="mlstm",
    )(q, k, v, o, g, gt, brow, bcol, c0, n0, m0, gh)


def _cumsum_table(n):
    j = np.arange(n)[:, None]
    s = np.arange(n)[None, :]
    half = np.concatenate([(j > s).astype(np.float32), np.ones((n, n), np.float32)], axis=1)
    return jnp.asarray(np.concatenate([half, half], axis=0), bf16)


def _sb_prompt_kernel(bias_ref, q_ref, k_ref, v_ref, uo_ref, o_ref, *, tq, lo):
    hp = pl.program_id(1)
    i = pl.program_id(2)
    lane = lax.broadcasted_iota(jnp.int32, (tq, LANES), 1)
    q2 = q_ref[0]
    zero = jnp.zeros_like(q2)
    qm = (jnp.where(lane < DH_B, q2, zero), jnp.where(lane >= DH_B, q2, zero))
    bias = (bias_ref[2 * hp], bias_ref[2 * hp + 1])
    row = lax.broadcasted_iota(jnp.int32, (tq, tq), 0)
    col = lax.broadcasted_iota(jnp.int32, (tq, tq), 1)
    dpos = row - col

    def sub_block(kb, first_valid, state):
        s0 = pl.multiple_of(kb * tq, tq)
        k2 = k_ref[0, pl.ds(s0, tq), :]
        v2 = v_ref[0, pl.ds(s0, tq), :]
        mask = (dpos > (kb - i) * tq) & (col >= first_valid - kb * tq)
        new = []
        for e in range(2):
            carry, acc = state[e]
            z = lax.dot_general(qm[e], k2, _NT, preferred_element_type=f32) + bias[e]
            ls = jnp.where(mask, -_softplus(z), 0.0)
            hi = ls.astype(bf16)
            lo_ = (ls - hi.astype(f32)).astype(bf16)
            r = jnp.dot(jnp.concatenate([hi, lo_], axis=1), uo_ref[...], preferred_element_type=f32)
            w = jnp.where(mask, jnp.exp(z + ls + r[:, :tq] + carry), 0.0)
            acc = acc + jnp.dot(w.astype(bf16), v2, preferred_element_type=f32)
            new.append((carry + r[:, tq:], acc))
        return tuple(new)

    zeros = jnp.zeros((tq, tq), f32)
    state = ((zeros, zeros), (zeros, zeros))
    state = sub_block(i, lo, state)

    def body(p, state):
        kb1 = i - 1 - 2 * p
        kb2 = kb1 - 1
        state = sub_block(kb1, lo, state)
        return sub_block(jnp.maximum(kb2, 0), jnp.where(kb2 >= 0, lo, 2 * tq), state)

    state = lax.fori_loop(0, (i + 1) // 2, body, state)
    o_ref[0] = jnp.where(lane < DH_B, state[0][1], state[1][1]).astype(o_ref.dtype)


def _sb_prompt(q, k, v, bias, lo):
    nb, length, width = q.shape
    tq = BLOCK
    blk = pl.BlockSpec((1, tq, LANES), lambda b, hp, i: (b, i, hp))
    full = pl.BlockSpec((1, length, LANES), lambda b, hp, i: (b, 0, hp))
    uo = _cumsum_table(tq)
    return pl.pallas_call(
        functools.partial(_sb_prompt_kernel, tq=tq, lo=lo),
        grid=(nb, width // LANES, length // tq),
        in_specs=[pl.BlockSpec(memory_space=pltpu.SMEM), blk, full, full,
                  pl.BlockSpec(uo.shape, lambda b, hp, i: (0, 0))],
        out_specs=blk,
        out_shape=jax.ShapeDtypeStruct((nb, length, width), bf16),
        compiler_params=_params("parallel", "parallel", "arbitrary"),
        name="sb_prompt",
    )(bias, q, k, v, uo)


PAGES_PER_TILE = 4
QH = 32


def _sb_sample_kernel(pt_ref, q_ref, kn_ref, vn_ref, bias_ref, uu_ref, *rest, n_pg, n_grp, n_tok):
    k_refs = rest[:n_pg]
    v_refs = rest[n_pg:2 * n_pg]
    o_ref, wb_sc, acc_sc, carry_sc = rest[2 * n_pg:]
    g = pl.program_id(1)
    width = H_B * DH_B
    psz = uu_ref.shape[0]
    row = lax.broadcasted_iota(jnp.int32, (LANES, width), 0)
    lanew = lax.broadcasted_iota(jnp.int32, (LANES, width), 1)
    l8 = lax.broadcasted_iota(jnp.int32, (SUBLANES, LANES), 1)

    def tile(k_list, v_list, mask):
        np_ = len(k_list)
        kcat = k_list[0] if np_ == 1 else jnp.concatenate(k_list, axis=1)
        z = lax.dot_general(kcat, wb_sc[:, :np_ * width], _NT, preferred_element_type=f32) + bias_ref[...]
        ls = -_softplus(z)
        if mask is not None:
            ls = jnp.where(mask, ls, 0.0)
        hi = ls.astype(bf16)
        lo_ = (ls - hi.astype(f32)).astype(bf16)
        within = jnp.dot(uu_ref[...], jnp.concatenate([hi, lo_], axis=0), preferred_element_type=f32)
        tot = jnp.broadcast_to(jnp.sum(ls, axis=0, keepdims=True), (SUBLANES, LANES))
        r1 = pltpu.roll(tot, LANES - QH, 1)
        r2 = pltpu.roll(tot, LANES - 2 * QH, 1)
        r3 = pltpu.roll(tot, LANES - 3 * QH, 1)
        later = (jnp.where(l8 < LANES - QH, r1, 0.0) + jnp.where(l8 < LANES - 2 * QH, r2, 0.0)
                 + jnp.where(l8 < LANES - 3 * QH, r3, 0.0))
        carry = carry_sc[...]
        w = jnp.exp(z + ls + within + (later + carry)[0:1, :])
        if mask is not None:
            w = jnp.where(mask, w, 0.0)
        carry_sc[...] = carry + (tot + r1 + r2 + r3)
        wt = w.T.astype(bf16)
        acc = acc_sc[...]
        for p in range(np_):
            acc = acc + jnp.dot(wt[p * QH:(p + 1) * QH, :], v_list[p], preferred_element_type=f32)
        acc_sc[...] = acc

    @pl.when(g == 0)
    def _():
        q = q_ref[0]
        slabs = [jnp.broadcast_to(q[t:t + 1, :], (SUBLANES, width))
                 for _ in range(PAGES_PER_TILE) for t in range(n_tok)]
        wq = jnp.concatenate(slabs, axis=0)
        wq = jnp.where(lanew // DH_B == row % H_B, wq, 0.0).astype(bf16)
        for p in range(PAGES_PER_TILE):
            wb_sc[:, p * width:(p + 1) * width] = jnp.where(row // QH == p, wq, jnp.zeros_like(wq))
        acc_sc[...] = jnp.zeros_like(acc_sc)
        carry_sc[...] = jnp.zeros_like(carry_sc)
        pad = jnp.zeros((psz - SUBLANES, width), f32)
        kn = jnp.concatenate([kn_ref[0], pad], axis=0).astype(bf16)
        vn = jnp.concatenate([vn_ref[0], pad], axis=0).astype(bf16)
        jrow = lax.broadcasted_iota(jnp.int32, (psz, LANES), 0)
        lcol = lax.broadcasted_iota(jnp.int32, (psz, LANES), 1)
        tile([kn], [vn], (lcol < QH) & (jrow < lcol // H_B))

    for half in reversed(range(n_pg // PAGES_PER_TILE)):
        ps = range(half * PAGES_PER_TILE, (half + 1) * PAGES_PER_TILE)
        tile([k_refs[p][0].astype(bf16) for p in ps], [v_refs[p][0].astype(bf16) for p in ps], None)

    @pl.when(g == n_grp - 1)
    def _():
        r32 = lax.broadcasted_iota(jnp.int32, (QH, width), 0)
        l32 = lax.broadcasted_iota(jnp.int32, (QH, width), 1)
        am = jnp.where(l32 // DH_B == r32 % H_B, acc_sc[...], 0.0)
        rows = [jnp.sum(am[t * H_B:(t + 1) * H_B, :], axis=0, keepdims=True) for t in range(n_tok)]
        rows.append(jnp.zeros((SUBLANES - n_tok, width), f32))
        o_ref[0] = jnp.concatenate(rows, axis=0).astype(o_ref.dtype)


def _sb_sample(q, kn, vn, bias_lane, cache_k, cache_v, page_table, n_tok, n_pg=8):
    db = q.shape[0]
    n_pages = page_table.shape[1]
    psz, width = cache_k.shape[1], cache_k.shape[2]
    n_grp = n_pages // n_pg
    assert n_tok * H_B == QH and psz == LANES and n_pages % n_pg == 0
    j = np.arange(psz)[None, :]
    s = np.arange(psz)[:, None]
    uu = jnp.asarray(np.tile((j > s).astype(np.float32), (1, 2)), bf16)

    def page_spec(p):
        return pl.BlockSpec((1, psz, width), lambda b, g, pt: (pt[b, (n_grp - 1 - g) * n_pg + p], 0, 0))

    small = pl.BlockSpec((1, SUBLANES, width), lambda b, g, pt: (b, 0, 0))
    grid_spec = pltpu.PrefetchScalarGridSpec(
        num_scalar_prefetch=1,
        grid=(db, n_grp),
        in_specs=[small, small, small,
                  pl.BlockSpec((1, LANES), lambda b, g, pt: (0, 0)),
                  pl.BlockSpec(uu.shape, lambda b, g, pt: (0, 0))]
        + [page_spec(p) for p in range(n_pg)] * 2,
        out_specs=small,
        scratch_shapes=[pltpu.VMEM((LANES, PAGES_PER_TILE * width), bf16),
                        pltpu.VMEM((QH, width), f32),
                        pltpu.VMEM((SUBLANES, LANES), f32)])
    return pl.pallas_call(
        functools.partial(_sb_sample_kernel, n_pg=n_pg, n_grp=n_grp, n_tok=n_tok),
        grid_spec=grid_spec,
        out_shape=jax.ShapeDtypeStruct((db, SUBLANES, width), bf16),
        compiler_params=_params("parallel", "arbitrary"),
        name="sb_sample",
    )(page_table, q, kn, vn, bias_lane, uu, *([cache_k] * n_pg), *([cache_v] * n_pg))


def _gla_tables(c):
    t = np.arange(c)[:, None]
    j = np.arange(c)[None, :]
    mats = [(j <= t), (j > t)]
    s = 1
    while s < c:
        bound = (t // (2 * s)) * (2 * s) + s - 1
        right = (t // s) % 2 == 1
        mats.append(np.where(right, (j > bound) & (j <= t), (j > t) & (j <= bound)))
        s *= 2
    return jnp.asarray(np.concatenate(mats, axis=0).astype(np.float32), bf16)


def _gla_kernel(q_ref, k_ref, v_ref, r_ref, gd_ref, wg_ref, bg_ref, e_ref, s0_ref, gh_ref,
                y_ref, S_ref, *, c, lo, hi):
    j = pl.program_id(1)

    @pl.when(j == 0)
    def _():
        S_ref[...] = s0_ref[...]

    pos = j * c + lax.broadcasted_iota(jnp.int32, (c, 1), 0)
    valid = (pos >= lo) & (pos < hi)
    rowc = lax.broadcasted_iota(jnp.int32, (c, 1), 0)
    row = lax.broadcasted_iota(jnp.int32, (c, c), 0)
    col = lax.broadcasted_iota(jnp.int32, (c, c), 1)
    lg = _log_sigmoid(jnp.dot(gd_ref[0].astype(bf16), wg_ref[...], preferred_element_type=f32)
                      + bg_ref[...]) / GATE_TAU
    lg = jnp.where(valid, lg, 0.0)
    n_lvl = e_ref.shape[0] // c - 2

    for h in range(H_C):
        sk = slice(h * DK_C, (h + 1) * DK_C)
        sv = slice(h * DV_C, (h + 1) * DV_C)
        q = q_ref[0, :, sk] * (DK_C ** -0.5)
        k = jnp.where(valid, k_ref[0, :, sk], 0.0)
        vb = v_ref[0, :, sv].astype(bf16)
        x = _dot01_left(e_ref[...], lg[:, sk])
        fx = jnp.exp(x)
        b_last = x[c - 1:c, :]
        a = jnp.where(row == col, jnp.sum(q * k, axis=1, keepdims=True), 0.0)
        for lvl in range(n_lvl):
            s = 1 << lvl
            fl = fx[(2 + lvl) * c:(3 + lvl) * c, :]
            right = (rowc // s) % 2 == 1
            qt = jnp.where(right, q * fl, 0.0).astype(bf16)
            kt = jnp.where(right, 0.0, k * fl).astype(bf16)
            p = lax.dot_general(qt, kt, _NT, preferred_element_type=f32)
            a = a + jnp.where(row // (2 * s) == col // (2 * s), p, 0.0)
        S = S_ref[0, h]
        o = (jnp.dot((q * fx[0:c, :]).astype(bf16), S.astype(bf16), preferred_element_type=f32)
             + jnp.dot(a.astype(bf16), vb, preferred_element_type=f32))
        decay = jnp.exp(jnp.transpose(jnp.broadcast_to(b_last, (DK_C, DK_C))))
        decay = jnp.concatenate([decay] * (DV_C // DK_C), axis=1)
        S_ref[0, h] = decay * S + lax.dot_general((k * fx[c:2 * c, :]).astype(bf16), vb, _TN,
                                                  preferred_element_type=f32)
        y = o * lax.rsqrt(jnp.mean(o * o, axis=1, keepdims=True) + EPS) * gh_ref[:, sv]
        y_ref[0, :, sv] = (y * jax.nn.silu(r_ref[0, :, sv])).astype(y_ref.dtype)


def _gla(q, k, v, r, gd, wg, bg, s0, gh, lo, hi):
    nb, length, _ = q.shape
    c = CHUNK_C
    e = _gla_tables(c)
    blk = lambda w: pl.BlockSpec((1, c, w), lambda b, j: (b, j, 0))
    const = lambda a: pl.BlockSpec(a.shape, lambda b, j: (0,) * a.ndim)
    st = pl.BlockSpec((1, H_C, DK_C, DV_C), lambda b, j: (b, 0, 0, 0))
    return pl.pallas_call(
        functools.partial(_gla_kernel, c=c, lo=lo, hi=hi),
        grid=(nb, length // c),
        in_specs=[blk(H_C * DK_C), blk(H_C * DK_C), blk(H_C * DV_C), blk(H_C * DV_C), blk(LANES),
                  const(wg), const(bg), const(e), st, const(gh)],
        out_specs=[blk(H_C * DV_C), st],
        out_shape=[jax.ShapeDtypeStruct((nb, length, H_C * DV_C), bf16),
                   jax.ShapeDtypeStruct((nb, H_C, DK_C, DV_C), f32)],
        compiler_params=_params("parallel", "arbitrary"),
        name="gla",
    )(q, k, v, r, gd, wg, bg, e, s0, gh)


def _pad_rows(x, rows):
    return jnp.pad(x, ((0, 0), (0, rows - x.shape[1]), (0, 0)))


def _row_tile(m):
    for tm in (512, 256, 128):
        if m % tm == 0:
            return tm
    raise ValueError(f"row count {m} is not a multiple of 128")


def kernel(x_prompt, x_sample, cache_sb_k, cache_sb_v, state_mlstm_c, state_mlstm_n, state_mlstm_m, state_gla_s,
           page_table, meta_tokens, norm_gains, w_in_even, b_gate_even, b_sb_even, g_head_even, w_out_even,
           w_in_odd, w_gate_up_odd, b_gate_up_odd, g_head_odd, w_out_odd, w_up, w_down):
    B, seq, D = x_prompt.shape
    DB, T, _ = x_sample.shape
    depth = norm_gains.shape[0]
    n_meta = meta_tokens.shape[0]
    meta_pad = BLOCK - n_meta
    Lp = meta_pad + n_meta + seq
    assert Lp % BLOCK == 0 and (DB * T) % LANES == 0 and T <= SUBLANES

    hp = jnp.concatenate([jnp.zeros((B, meta_pad, D), f32),
                          jnp.broadcast_to(meta_tokens[None], (B, n_meta, D)), x_prompt], axis=1)
    hp = hp.reshape(B * Lp, D)
    hs = x_sample.reshape(DB * T, D)
    tm_p, tm_s = _row_tile(B * Lp), _row_tile(DB * T)
    tm_in = min(tm_p, 256)

    na, nb_ = H_A * DK_A, H_B * DH_B
    nk, nv = H_C * DK_C, H_C * DV_C
    o_qa, o_ka, o_va, o_oa = 0, na, 2 * na, 3 * na
    o_qb, o_kb, o_vb, o_g = 4 * na, 4 * na + nb_, 4 * na + 2 * nb_, 4 * na + 3 * nb_
    even_outs = ((o_qa, na, 1.0), (o_ka, na, 1.0), (o_va, na, 1.0), (o_oa, na, 1.0), (o_g, LANES, 1.0),
                 (o_kb, nb_, 1.0), (o_vb, nb_, 1.0),
                 (o_qb, nb_, DH_B ** -0.5), (o_kb, nb_, 1.0), (o_vb, nb_, 1.0))
    even_dt = (f32,) * 7 + (bf16,) * 3
    odd_outs = ((0, nk, 1.0), (nk, nk, 1.0), (2 * nk, nv, 1.0), (2 * nk + nv, nv, 1.0), (2 * nk + 2 * nv, LANES, 1.0))
    odd_dt = (f32,) * 5

    outs = {n: [] for n in ("pk", "pv", "pc", "pn", "pm", "ps", "sk", "sv", "sc", "sn", "sm", "ss")}

    for layer in range(depth):
        gains = norm_gains[layer]
        if layer % 2 == 0:
            e = layer // 2
            w = w_in_even[e]
            w = jnp.concatenate([w[:, :4 * na], w[:, 4 * na + 2 * H_A:], w[:, 4 * na:4 * na + 2 * H_A],
                                 jnp.zeros((D, LANES - 2 * H_A), f32)], axis=1).astype(bf16)
            brow = jnp.pad(b_gate_even[e], (0, LANES - 2 * H_A)).reshape(1, LANES)
            bcol = b_gate_even[e].reshape(2 * H_A, 1)
            gh = g_head_even[e].reshape(1, -1)
            w_out = w_out_even[e].astype(bf16)

            qa, ka, va, oa, gt, kb, vb, qb16, kb16, vb16 = _in_proj(hp, gains[0:1], w, even_outs, even_dt, tm_in)
            r3 = lambda a: a.reshape(B, Lp, a.shape[-1])
            g3 = r3(gt)
            ya, C, n, m = _mlstm(r3(qa), r3(ka), r3(va), r3(oa), g3, jnp.swapaxes(g3[:, :, :2 * H_A], 1, 2),
                                 brow, bcol, jnp.zeros((B, H_A, DK_A, DV_A), f32), jnp.zeros((B, H_A, DK_A), f32),
                                 jnp.zeros((B, H_A, 1), f32), gh, meta_pad, Lp)
            hb = _sb_prompt(r3(qb16), r3(kb16), r3(vb16), b_sb_even[e], meta_pad)
            hp = _post([ya.reshape(B * Lp, -1), hb.reshape(B * Lp, -1)], hp, w_out, gains,
                       w_up[layer].astype(bf16), w_down[layer].astype(bf16), tm_p, 512)
            outs["pk"].append(kb.reshape(B, Lp, H_B, DH_B)[:, meta_pad:])
            outs["pv"].append(vb.reshape(B, Lp, H_B, DH_B)[:, meta_pad:])
            outs["pc"].append(C)
            outs["pn"].append(n)
            outs["pm"].append(m.reshape(B, H_A))

            qa, ka, va, oa, gt, kb, vb, qb16, kb16, vb16 = _in_proj(hs, gains[0:1], w, even_outs, even_dt, tm_s)
            s3 = lambda a: a.reshape(DB, T, a.shape[-1])
            pc = lambda a: _pad_rows(s3(a), CHUNK_A)
            g3 = pc(gt)
            ya, C, n, m = _mlstm(pc(qa), pc(ka), pc(va), pc(oa), g3, jnp.swapaxes(g3[:, :, :2 * H_A], 1, 2),
                                 brow, bcol, state_mlstm_c[e], state_mlstm_n[e],
                                 state_mlstm_m[e].reshape(DB, H_A, 1), gh, 0, T)
            p8 = lambda a: _pad_rows(s3(a), SUBLANES)
            n_pool = cache_sb_k.shape[1]
            psz = cache_sb_k.shape[2]
            hb = _sb_sample(p8(qb16.astype(f32)), p8(kb), p8(vb),
                            jnp.tile(b_sb_even[e], LANES // H_B).reshape(1, LANES),
                            cache_sb_k[e].reshape(n_pool, psz, nb_), cache_sb_v[e].reshape(n_pool, psz, nb_),
                            page_table, T)
            hs = _post([ya[:, :T].reshape(DB * T, -1), hb[:, :T].reshape(DB * T, -1)], hs, w_out, gains,
                       w_up[layer].astype(bf16), w_down[layer].astype(bf16), tm_s, 512)
            outs["sk"].append(kb.reshape(DB, T, H_B, DH_B))
            outs["sv"].append(vb.reshape(DB, T, H_B, DH_B))
            outs["sc"].append(C)
            outs["sn"].append(n)
            outs["sm"].append(m.reshape(DB, H_A))
        else:
            o = layer // 2
            w = jnp.concatenate([w_in_odd[o], jnp.zeros((D, LANES - GATE_RANK), f32)], axis=1).astype(bf16)
            wg = jnp.concatenate([w_gate_up_odd[o], jnp.zeros((LANES - GATE_RANK, nk), f32)], axis=0).astype(bf16)
            bg = b_gate_up_odd[o].reshape(1, nk)
            gh = g_head_odd[o].reshape(1, -1)
            w_out = w_out_odd[o].astype(bf16)

            q, k, v, r, gd = _in_proj(hp, gains[0:1], w, odd_outs, odd_dt, tm_in)
            r3 = lambda a: a.reshape(B, Lp, a.shape[-1])
            y, S = _gla(r3(q), r3(k), r3(v), r3(r), r3(gd), wg, bg, jnp.zeros((B, H_C, DK_C, DV_C), f32), gh,
                        meta_pad, Lp)
            hp = _post([y.reshape(B * Lp, -1)], hp, w_out, gains,
                       w_up[layer].astype(bf16), w_down[layer].astype(bf16), tm_p, 512)
            outs["ps"].append(S)

            q, k, v, r, gd = _in_proj(hs, gains[0:1], w, odd_outs, odd_dt, tm_s)
            pc = lambda a: _pad_rows(a.reshape(DB, T, a.shape[-1]), CHUNK_C)
            y, S = _gla(pc(q), pc(k), pc(v), pc(r), pc(gd), wg, bg, state_gla_s[o], gh, 0, T)
            hs = _post([y[:, :T].reshape(DB * T, -1)], hs, w_out, gains,
                       w_up[layer].astype(bf16), w_down[layer].astype(bf16), tm_s, 512)
            outs["ss"].append(S)

    y_prompt = hp.reshape(B, Lp, D)[:, meta_pad + n_meta:]
    y_sample = hs.reshape(DB, T, D)
    st = jnp.stack
    return (y_prompt, y_sample, st(outs["pk"]), st(outs["pv"]), st(outs["pc"]), st(outs["pn"]), st(outs["pm"]),
            st(outs["ps"]), st(outs["sk"]), st(outs["sv"]), st(outs["sc"]), st(outs["sn"]), st(outs["sm"]),
            st(outs["ss"]))
```

```python
import functools

import jax
import jax.numpy as jnp
import numpy as np
from jax import lax
from jax.experimental import pallas as pl
from jax.experimental.pallas import tpu as pltpu

f32 = jnp.float32
bf16 = jnp.bfloat16

BLOCK = 128
H_A, DK_A, DV_A = 4, 128, 128
H_B, DH_B = 8, 64
H_C, DK_C, DV_C = 4, 128, 256
GATE_RANK = 16
GATE_TAU = 16.0
CHUNK_A = 128
CHUNK_C = 64
EPS = 1e-6

LANES = 128
SUBLANES = 8
VMEM_LIMIT = 48 * 1024 * 1024

_NT = (((1,), (1,)), ((), ()))
_TN = (((0,), (0,)), ((), ()))


def _rms(x, g):
    return x * lax.rsqrt(jnp.mean(x * x, axis=-1, keepdims=True) + EPS) * g


def _softplus(z):
    return jnp.maximum(z, 0.0) + jnp.log1p(jnp.exp(-jnp.abs(z)))


def _log_sigmoid(z):
    return jnp.minimum(z, 0.0) - jnp.log1p(jnp.exp(-jnp.abs(z)))


def _split3(x):
    hi = x.astype(bf16)
    r = x - hi.astype(f32)
    mid = r.astype(bf16)
    lo = (r - mid.astype(f32)).astype(bf16)
    return hi, mid, lo


def _dot01_left(m01, x):
    return sum(jnp.dot(m01, p, preferred_element_type=f32) for p in _split3(x))


def _dot01_right(x, m01):
    return sum(jnp.dot(p, m01, preferred_element_type=f32) for p in _split3(x))


def _params(*sem):
    return pltpu.CompilerParams(dimension_semantics=sem, vmem_limit_bytes=VMEM_LIMIT)


def _in_proj_kernel(x_ref, g_ref, w_ref, *o_refs, outs):
    xn = _rms(x_ref[...], g_ref[...]).astype(bf16)
    cache = {}
    for o_ref, (off, n, scale) in zip(o_refs, outs):
        if (off, n) not in cache:
            cache[(off, n)] = jnp.dot(xn, w_ref[:, off:off + n], preferred_element_type=f32)
        y = cache[(off, n)]
        if scale != 1.0:
            y = y * scale
        o_ref[...] = y.astype(o_ref.dtype)


def _in_proj(x, g, w, outs, dtypes, tm):
    m, d = x.shape
    n_tot = w.shape[1]
    return pl.pallas_call(
        functools.partial(_in_proj_kernel, outs=outs),
        grid=(m // tm,),
        in_specs=[pl.BlockSpec((tm, d), lambda i: (i, 0)),
                  pl.BlockSpec((1, d), lambda i: (0, 0)),
                  pl.BlockSpec((d, n_tot), lambda i: (0, 0))],
        out_specs=[pl.BlockSpec((tm, n), lambda i: (i, 0)) for (_, n, _) in outs],
        out_shape=[jax.ShapeDtypeStruct((m, n), dt) for (_, n, _), dt in zip(outs, dtypes)],
        compiler_params=_params("parallel"),
        name="in_proj",
    )(x, g, w)


def _post_kernel(*refs, n_a, nf):
    a_refs = refs[:n_a]
    h_ref, wo_ref, g_ref, wu_ref, wd_ref, out_ref, h1_sc, xn_sc, acc_sc = refs[n_a:]
    f = pl.program_id(1)

    @pl.when(f == 0)
    def _():
        if n_a > 1:
            a = jnp.concatenate([r[...] for r in a_refs], axis=-1)
        else:
            a = a_refs[0][...]
        mix = jnp.dot(a, wo_ref[...], preferred_element_type=f32)
        h1 = h_ref[...] + _rms(mix, g_ref[1:2, :])
        h1_sc[...] = h1
        xn_sc[...] = _rms(h1, g_ref[2:3, :]).astype(bf16)
        acc_sc[...] = jnp.zeros_like(acc_sc)

    u = jnp.dot(xn_sc[...], wu_ref[...], preferred_element_type=f32)
    u = jnp.maximum(u, 0.0)
    acc_sc[...] += jnp.dot((u * u).astype(bf16), wd_ref[...], preferred_element_type=f32)

    @pl.when(f == nf - 1)
    def _():
        out_ref[...] = h1_sc[...] + _rms(acc_sc[...], g_ref[3:4, :])


def _post(a_list, h, w_out, gains, w_up, w_down, tm, tf):
    m, d = h.shape
    dff = w_up.shape[1]
    nf = dff // tf
    n_a = len(a_list)
    return pl.pallas_call(
        functools.partial(_post_kernel, n_a=n_a, nf=nf),
        grid=(m // tm, nf),
        in_specs=[pl.BlockSpec((tm, a.shape[1]), lambda i, f: (i, 0)) for a in a_list] + [
            pl.BlockSpec((tm, d), lambda i, f: (i, 0)),
            pl.BlockSpec(w_out.shape, lambda i, f: (0, 0)),
            pl.BlockSpec(gains.shape, lambda i, f: (0, 0)),
            pl.BlockSpec((d, tf), lambda i, f: (0, f)),
            pl.BlockSpec((tf, d), lambda i, f: (f, 0))],
        out_specs=pl.BlockSpec((tm, d), lambda i, f: (i, 0)),
        out_shape=jax.ShapeDtypeStruct((m, d), f32),
        scratch_shapes=[pltpu.VMEM((tm, d), f32), pltpu.VMEM((tm, d), bf16), pltpu.VMEM((tm, d), f32)],
        compiler_params=_params("parallel", "arbitrary"),
        name="post",
    )(*a_list, h, w_out, gains, w_up, w_down)


def _mlstm_kernel(q_ref, k_ref, v_ref, o_ref, g_ref, gt_ref, brow_ref, bcol_ref, c0_ref, n0_ref, m0_ref, gh_ref,
                  y_ref, C_ref, n_ref, m_ref, *, c, lo, hi):
    j = pl.program_id(1)

    @pl.when(j == 0)
    def _():
        C_ref[...] = c0_ref[...]
        n_ref[...] = n0_ref[...]
        m_ref[...] = m0_ref[...]

    row = lax.broadcasted_iota(jnp.int32, (c, c), 0)
    col = lax.broadcasted_iota(jnp.int32, (c, c), 1)
    causal = row >= col
    tril = jnp.where(causal, 1.0, 0.0).astype(bf16)
    triu = jnp.where(row <= col, 1.0, 0.0).astype(bf16)
    pos_c = j * c + lax.broadcasted_iota(jnp.int32, (c, 1), 0)
    valid_c = (pos_c >= lo) & (pos_c < hi)
    pos_r = j * c + lax.broadcasted_iota(jnp.int32, (1, c), 1)
    valid_r = (pos_r >= lo) & (pos_r < hi)

    gc = g_ref[0] + brow_ref[...]
    li_c = jnp.where(valid_c, gc, -jnp.inf)
    lf_c = jnp.where(valid_c, _log_sigmoid(gc), 0.0)
    b_c = _dot01_left(tril, lf_c)
    gr = gt_ref[0] + bcol_ref[...]
    li_r = jnp.where(valid_r, gr, -jnp.inf)
    lf_r = jnp.where(valid_r, _log_sigmoid(gr), 0.0)
    b_r = _dot01_right(lf_r, triu)

    for h in range(H_A):
        sl = slice(h * DK_A, (h + 1) * DK_A)
        q = q_ref[0, :, sl]
        k = k_ref[0, :, sl] * (DK_A ** -0.5)
        qb, kb, vb = q.astype(bf16), k.astype(bf16), v_ref[0, :, sl].astype(bf16)
        bc = b_c[:, H_A + h:H_A + h + 1]
        ic = li_c[:, h:h + 1]
        br = b_r[H_A + h:H_A + h + 1, :]
        ir = li_r[h:h + 1, :]
        m = m_ref[0, h:h + 1, :]
        C = C_ref[0, h]
        n = n_ref[0, h:h + 1, :]

        d_log = jnp.where(causal, bc - br + ir, -jnp.inf)
        inter = bc + m
        m_t = jnp.maximum(inter, jnp.max(d_log, axis=1, keepdims=True))
        s = lax.dot_general(qb, kb, _NT, preferred_element_type=f32) * jnp.exp(d_log - m_t)
        w_i = jnp.exp(inter - m_t)
        num = (w_i * jnp.dot(qb, C.astype(bf16), preferred_element_type=f32)
               + jnp.dot(s.astype(bf16), vb, preferred_element_type=f32))
        den = w_i * jnp.sum(q * n, axis=1, keepdims=True) + jnp.sum(s, axis=1, keepdims=True)
        hh = num / jnp.maximum(jnp.abs(den), jnp.exp(-m_t))

        b_last = bc[c - 1:c, :]
        gcol = b_last - bc + ic
        m_new = jnp.maximum(b_last + m, jnp.max(gcol, axis=0, keepdims=True))
        w_d = jnp.exp(b_last + m - m_new)
        kw = k * jnp.exp(gcol - m_new)
        C_ref[0, h] = w_d * C + lax.dot_general(kw.astype(bf16), vb, _TN, preferred_element_type=f32)
        n_ref[0, h:h + 1, :] = w_d * n + jnp.sum(kw, axis=0, keepdims=True)
        m_ref[0, h:h + 1, :] = m_new

        y = hh * lax.rsqrt(jnp.mean(hh * hh, axis=1, keepdims=True) + EPS) * gh_ref[:, sl]
        y_ref[0, :, sl] = (y * jax.nn.sigmoid(o_ref[0, :, sl])).astype(y_ref.dtype)


def _mlstm(q, k, v, o, g, gt, brow, bcol, c0, n0, m0, gh, lo, hi):
    nb, length, _ = q.shape
    c = CHUNK_A
    wide = pl.BlockSpec((1, c, H_A * DK_A), lambda b, j: (b, j, 0))
    st = lambda shape: pl.BlockSpec((1,) + shape, lambda b, j: (b,) + (0,) * len(shape))
    const = lambda a: pl.BlockSpec(a.shape, lambda b, j: (0,) * a.ndim)
    return pl.pallas_call(
        functools.partial(_mlstm_kernel, c=c, lo=lo, hi=hi),
        grid=(nb, length // c),
        in_specs=[wide, wide, wide, wide,
                  pl.BlockSpec((1, c, LANES), lambda b, j: (b, j, 0)),
                  pl.BlockSpec((1, 2 * H_A, c), lambda b, j: (b, 0, j)),
                  const(brow), const(bcol),
                  st((H_A, DK_A, DV_A)), st((H_A, DK_A)), st((H_A, 1)), const(gh)],
        out_specs=[wide, st((H_A, DK_A, DV_A)), st((H_A, DK_A)), st((H_A, 1))],
        out_shape=[jax.ShapeDtypeStruct((nb, length, H_A * DV_A), bf16),
                   jax.ShapeDtypeStruct((nb, H_A, DK_A, DV_A), f32),
                   jax.ShapeDtypeStruct((nb, H_A, DK_A), f32),
                   jax.ShapeDtypeStruct((nb, H_A, 1), f32)],
        compiler_params=_params("parallel", "arbitrary"),
        name="mlstm",
    )(q, k, v, o, g, gt, brow, bcol, c0, n0, m0, gh)


SB_HEADS = 4


def _neg_cumsum_table(n):
    j = np.arange(n)[:, None]
    s = np.arange(n)[None, :]
    half = np.concatenate([(j > s).astype(np.float32), np.ones((n, n), np.float32)], axis=1)
    return jnp.asarray(-np.concatenate([half, half], axis=0), bf16)


def _sb_prompt_kernel(bias_ref, q_ref, k_ref, v_ref, tab_ref, o_ref, carry_ref, acc_ref, *, tq, nsub, lo):
    hg = pl.program_id(1)
    i = pl.program_id(2)
    wt = nsub * tq
    lw = SB_HEADS * DH_B
    lane = lax.broadcasted_iota(jnp.int32, (tq, lw), 1)
    qg = q_ref[0]
    qm = [jnp.where(lane // DH_B == e, qg, jnp.zeros_like(qg)) for e in range(SB_HEADS)]
    bias = [bias_ref[SB_HEADS * hg + e] for e in range(SB_HEADS)]
    carry_ref[...] = jnp.zeros_like(carry_ref)
    acc_ref[...] = jnp.zeros_like(acc_ref)

    def tile(tau, key_bias):
        s0 = pl.multiple_of(tau * wt, wt)
        kt = k_ref[0, pl.ds(s0, wt), :]
        vt = v_ref[0, pl.ds(s0, wt), :]
        heads = range(SB_HEADS)
        z = [lax.dot_general(qm[e], kt, _NT, preferred_element_type=f32) + key_bias[e] for e in heads]
        sp = [jnp.maximum(z[e], 0.0) + jnp.log(1.0 + jnp.exp(-jnp.abs(z[e]))) for e in heads]
        hi = [sp[e].astype(bf16) for e in heads]
        lo_ = [(sp[e] - hi[e].astype(f32)).astype(bf16) for e in heads]
        r = [[jnp.dot(jnp.concatenate([hi[e][:, j * tq:(j + 1) * tq], lo_[e][:, j * tq:(j + 1) * tq]], axis=1),
                      tab_ref[...], preferred_element_type=f32) for j in range(nsub)] for e in heads]
        t = [z[e] - sp[e] for e in heads]
        for e in heads:
            c = carry_ref[e]
            after = [None] * nsub
            for j in reversed(range(nsub)):
                after[j] = r[e][j][:, :tq] + c
                c = c + r[e][j][:, tq:]
            carry_ref[e] = c
            t[e] = t[e] + jnp.concatenate(after, axis=1)
        w = [jnp.exp(t[e]).astype(bf16) for e in heads]
        for e in heads:
            acc_ref[e] += jnp.dot(w[e], vt, preferred_element_type=f32)

    top = i // nsub
    row = lax.broadcasted_iota(jnp.int32, (tq, wt), 0)
    col = lax.broadcasted_iota(jnp.int32, (tq, wt), 1)
    spos = top * wt + col
    readable = (spos < i * tq + row) & (spos >= lo)
    tile(top, [jnp.where(readable, b, -jnp.inf) for b in bias])

    def body(n, carry):
        tau = top - 1 - n
        spos = tau * wt + lax.broadcasted_iota(jnp.int32, (1, wt), 1)
        tile(tau, [jnp.where(spos >= lo, b, -jnp.inf) for b in bias])
        return carry

    lax.fori_loop(0, top, body, 0)
    out = acc_ref[0]
    for e in range(1, SB_HEADS):
        out = jnp.where(lane // DH_B == e, acc_ref[e], out)
    o_ref[0] = out.astype(o_ref.dtype)


def _sb_prompt(q, k, v, bias, lo):
    nb, length, width = q.shape
    tq = BLOCK
    nblk = length // tq
    nsub = max(d for d in (1, 2, 3, 4) if nblk % d == 0)
    lw = SB_HEADS * DH_B
    blk = pl.BlockSpec((1, tq, lw), lambda b, hg, i: (b, i, hg))
    full = pl.BlockSpec((1, length, lw), lambda b, hg, i: (b, 0, hg))
    tab = _neg_cumsum_table(tq)
    return pl.pallas_call(
        functools.partial(_sb_prompt_kernel, tq=tq, nsub=nsub, lo=lo),
        grid=(nb, width // lw, nblk),
        in_specs=[pl.BlockSpec(memory_space=pltpu.SMEM), blk, full, full,
                  pl.BlockSpec(tab.shape, lambda b, hg, i: (0, 0))],
        out_specs=blk,
        out_shape=jax.ShapeDtypeStruct((nb, length, width), bf16),
        scratch_shapes=[pltpu.VMEM((SB_HEADS, tq, tq), f32), pltpu.VMEM((SB_HEADS, tq, lw), f32)],
        compiler_params=_params("parallel", "parallel", "arbitrary"),
        name="sb_prompt",
    )(bias, q, k, v, tab)


QH = 32


def _head_cumsum_table():
    a = np.arange(LANES)
    same = (a[:, None] % H_B) == (a[None, :] % H_B)
    later = (a[:, None] // H_B) > (a[None, :] // H_B)
    half = np.concatenate([(same & later).astype(np.float32), same.astype(np.float32)], axis=1)
    return jnp.asarray(-np.concatenate([half, half], axis=0), bf16)


def _sb_sample_kernel(pt_ref, q_ref, kn_ref, vn_ref, bias_ref, tab_ref, *rest, n_pg, n_grp):
    k_refs = rest[:n_pg]
    v_refs = rest[n_pg:2 * n_pg]
    o_ref, carry_sc, acc_sc = rest[2 * n_pg:]
    g = pl.program_id(1)
    qb = q_ref[0].astype(bf16)

    def sweep(pages, bias):
        zs = [lax.dot_general(qb, kf, _NT, preferred_element_type=f32) for kf, _ in pages]
        tiles = [(p, j) for p in range(len(pages)) for j in range(zs[p].shape[1] // LANES)]
        z = jnp.concatenate([zs[p][:, j * LANES:(j + 1) * LANES] + bias for p, j in tiles], axis=0)
        sp = jnp.maximum(z, 0.0) + jnp.log(1.0 + jnp.exp(-jnp.abs(z)))
        hi = sp.astype(bf16)
        lo_ = (sp - hi.astype(f32)).astype(bf16)
        r = jnp.dot(jnp.concatenate([hi, lo_], axis=1), tab_ref[...], preferred_element_type=f32)
        c = carry_sc[...]
        after = [None] * len(tiles)
        for n in reversed(range(len(tiles))):
            after[n] = r[n * QH:(n + 1) * QH, :LANES] + c
            c = c + r[n * QH:(n + 1) * QH, LANES:]
        carry_sc[...] = c
        w = jnp.exp(z - sp + jnp.concatenate(after, axis=0)).astype(bf16)
        acc = acc_sc[...]
        for p, (_, vf) in enumerate(pages):
            wp = jnp.concatenate([w[n * QH:(n + 1) * QH, :] for n, (pp, _) in enumerate(tiles) if pp == p], axis=1)
            acc = acc + jnp.dot(wp, vf, preferred_element_type=f32)
        acc_sc[...] = acc

    @pl.when(g == 0)
    def _():
        acc_sc[...] = jnp.zeros_like(acc_sc)
        carry_sc[...] = jnp.zeros_like(carry_sc)
        j = lax.broadcasted_iota(jnp.int32, (QH, LANES), 1) // H_B
        t = lax.broadcasted_iota(jnp.int32, (QH, LANES), 0) // H_B
        sweep([(kn_ref[0].astype(bf16), vn_ref[0].astype(bf16))], jnp.where(j < t, bias_ref[...], -jnp.inf))

    sweep([(k_refs[p][0, 0].astype(bf16), v_refs[p][0, 0].astype(bf16)) for p in range(n_pg)], bias_ref[...])

    @pl.when(g == n_grp - 1)
    def _():
        o_ref[0] = acc_sc[...].astype(o_ref.dtype)


def _sb_sample(q, kn, vn, bias, cache_k, cache_v, page_table, layer, n_pg=8):
    db = q.shape[0]
    n_pages = page_table.shape[1]
    n_layers, n_pool, psz = cache_k.shape[:3]
    n_grp = n_pages // n_pg
    assert q.shape[1] == QH and (psz * H_B) % LANES == 0 and n_pages % n_pg == 0
    tab = _head_cumsum_table()
    cache_k = cache_k.reshape(n_layers, n_pool, psz * H_B, DH_B)
    cache_v = cache_v.reshape(n_layers, n_pool, psz * H_B, DH_B)

    def page_spec(p):
        return pl.BlockSpec((1, 1, psz * H_B, DH_B),
                            lambda b, g, pt: (layer, pt[b, (n_grp - 1 - g) * n_pg + p], 0, 0))

    new_spec = pl.BlockSpec((1,) + kn.shape[1:], lambda b, g, pt: (b, 0, 0))
    qspec = pl.BlockSpec((1, QH, DH_B), lambda b, g, pt: (b, 0, 0))
    grid_spec = pltpu.PrefetchScalarGridSpec(
        num_scalar_prefetch=1,
        grid=(db, n_grp),
        in_specs=[qspec, new_spec, new_spec,
                  pl.BlockSpec(bias.shape, lambda b, g, pt: (0, 0)),
                  pl.BlockSpec(tab.shape, lambda b, g, pt: (0, 0))]
        + [page_spec(p) for p in range(n_pg)] * 2,
        out_specs=qspec,
        scratch_shapes=[pltpu.VMEM((QH, LANES), f32), pltpu.VMEM((QH, DH_B), f32)])
    return pl.pallas_call(
        functools.partial(_sb_sample_kernel, n_pg=n_pg, n_grp=n_grp),
        grid_spec=grid_spec,
        out_shape=jax.ShapeDtypeStruct((db, QH, DH_B), bf16),
        compiler_params=_params("parallel", "arbitrary"),
        name="sb_sample",
    )(page_table, q, kn, vn, bias, tab, *([cache_k] * n_pg), *([cache_v] * n_pg))


def _gla_tables(c):
    t = np.arange(c)[:, None]
    j = np.arange(c)[None, :]
    mats = [(j <= t), (j > t)]
    s = 1
    while s < c:
        bound = (t // (2 * s)) * (2 * s) + s - 1
        right = (t // s) % 2 == 1
        mats.append(np.where(right, (j > bound) & (j <= t), (j > t) & (j <= bound)))
        s *= 2
    return jnp.asarray(np.concatenate(mats, axis=0).astype(np.float32), bf16)


def _gla_kernel(q_ref, k_ref, v_ref, r_ref, gd_ref, wg_ref, bg_ref, e_ref, s0_ref, gh_ref,
                y_ref, S_ref, *, c, lo, hi):
    j = pl.program_id(1)

    @pl.when(j == 0)
    def _():
        S_ref[...] = s0_ref[...]

    pos = j * c + lax.broadcasted_iota(jnp.int32, (c, 1), 0)
    valid = (pos >= lo) & (pos < hi)
    rowc = lax.broadcasted_iota(jnp.int32, (c, 1), 0)
    row = lax.broadcasted_iota(jnp.int32, (c, c), 0)
    col = lax.broadcasted_iota(jnp.int32, (c, c), 1)
    lg = _log_sigmoid(jnp.dot(gd_ref[0].astype(bf16), wg_ref[...], preferred_element_type=f32)
                      + bg_ref[...]) / GATE_TAU
    lg = jnp.where(valid, lg, 0.0)
    n_lvl = e_ref.shape[0] // c - 2

    for h in range(H_C):
        sk = slice(h * DK_C, (h + 1) * DK_C)
        sv = slice(h * DV_C, (h + 1) * DV_C)
        q = q_ref[0, :, sk] * (DK_C ** -0.5)
        k = jnp.where(valid, k_ref[0, :, sk], 0.0)
        vb = v_ref[0, :, sv].astype(bf16)
        x = _dot01_left(e_ref[...], lg[:, sk])
        fx = jnp.exp(x)
        b_last = x[c - 1:c, :]
        a = jnp.where(row == col, jnp.sum(q * k, axis=1, keepdims=True), 0.0)
        for lvl in range(n_lvl):
            s = 1 << lvl
            fl = fx[(2 + lvl) * c:(3 + lvl) * c, :]
            right = (rowc // s) % 2 == 1
            qt = jnp.where(right, q * fl, 0.0).astype(bf16)
            kt = jnp.where(right, 0.0, k * fl).astype(bf16)
            p = lax.dot_general(qt, kt, _NT, preferred_element_type=f32)
            a = a + jnp.where(row // (2 * s) == col // (2 * s), p, 0.0)
        S = S_ref[0, h]
        o = (jnp.dot((q * fx[0:c, :]).astype(bf16), S.astype(bf16), preferred_element_type=f32)
             + jnp.dot(a.astype(bf16), vb, preferred_element_type=f32))
        decay = jnp.exp(jnp.transpose(jnp.broadcast_to(b_last, (DK_C, DK_C))))
        decay = jnp.concatenate([decay] * (DV_C // DK_C), axis=1)
        S_ref[0, h] = decay * S + lax.dot_general((k * fx[c:2 * c, :]).astype(bf16), vb, _TN,
                                                  preferred_element_type=f32)
        y = o * lax.rsqrt(jnp.mean(o * o, axis=1, keepdims=True) + EPS) * gh_ref[:, sv]
        y_ref[0, :, sv] = (y * jax.nn.silu(r_ref[0, :, sv])).astype(y_ref.dtype)


def _gla(q, k, v, r, gd, wg, bg, s0, gh, lo, hi):
    nb, length, _ = q.shape
    c = CHUNK_C
    e = _gla_tables(c)
    blk = lambda w: pl.BlockSpec((1, c, w), lambda b, j: (b, j, 0))
    const = lambda a: pl.BlockSpec(a.shape, lambda b, j: (0,) * a.ndim)
    st = pl.BlockSpec((1, H_C, DK_C, DV_C), lambda b, j: (b, 0, 0, 0))
    return pl.pallas_call(
        functools.partial(_gla_kernel, c=c, lo=lo, hi=hi),
        grid=(nb, length // c),
        in_specs=[blk(H_C * DK_C), blk(H_C * DK_C), blk(H_C * DV_C), blk(H_C * DV_C), blk(LANES),
                  const(wg), const(bg), const(e), st, const(gh)],
        out_specs=[blk(H_C * DV_C), st],
        out_shape=[jax.ShapeDtypeStruct((nb, length, H_C * DV_C), bf16),
                   jax.ShapeDtypeStruct((nb, H_C, DK_C, DV_C), f32)],
        compiler_params=_params("parallel", "arbitrary"),
        name="gla",
    )(q, k, v, r, gd, wg, bg, e, s0, gh)


def _pad_rows(x, rows):
    return jnp.pad(x, ((0, 0), (0, rows - x.shape[1]), (0, 0)))


def _row_tile(m):
    for tm in (512, 256, 128):
        if m % tm == 0:
            return tm
    raise ValueError(f"row count {m} is not a multiple of 128")


def kernel(x_prompt, x_sample, cache_sb_k, cache_sb_v, state_mlstm_c, state_mlstm_n, state_mlstm_m, state_gla_s,
           page_table, meta_tokens, norm_gains, w_in_even, b_gate_even, b_sb_even, g_head_even, w_out_even,
           w_in_odd, w_gate_up_odd, b_gate_up_odd, g_head_odd, w_out_odd, w_up, w_down):
    B, seq, D = x_prompt.shape
    DB, T, _ = x_sample.shape
    depth = norm_gains.shape[0]
    n_meta = meta_tokens.shape[0]
    meta_pad = BLOCK - n_meta
    Lp = meta_pad + n_meta + seq
    assert Lp % BLOCK == 0 and (DB * T) % LANES == 0 and T <= SUBLANES

    hp = jnp.concatenate([jnp.zeros((B, meta_pad, D), f32),
                          jnp.broadcast_to(meta_tokens[None], (B, n_meta, D)), x_prompt], axis=1)
    hp = hp.reshape(B * Lp, D)
    hs = x_sample.reshape(DB * T, D)
    tm_p, tm_s = _row_tile(B * Lp), _row_tile(DB * T)
    tm_in = min(tm_p, 256)

    na, nb_ = H_A * DK_A, H_B * DH_B
    nk, nv = H_C * DK_C, H_C * DV_C
    o_qa, o_ka, o_va, o_oa = 0, na, 2 * na, 3 * na
    o_qb, o_kb, o_vb, o_g = 4 * na, 4 * na + nb_, 4 * na + 2 * nb_, 4 * na + 3 * nb_
    even_outs = ((o_qa, na, 1.0), (o_ka, na, 1.0), (o_va, na, 1.0), (o_oa, na, 1.0), (o_g, LANES, 1.0),
                 (o_kb, nb_, 1.0), (o_vb, nb_, 1.0),
                 (o_qb, nb_, DH_B ** -0.5), (o_kb, nb_, 1.0), (o_vb, nb_, 1.0))
    even_dt = (f32,) * 7 + (bf16,) * 3
    odd_outs = ((0, nk, 1.0), (nk, nk, 1.0), (2 * nk, nv, 1.0), (2 * nk + nv, nv, 1.0), (2 * nk + 2 * nv, LANES, 1.0))
    odd_dt = (f32,) * 5

    outs = {n: [] for n in ("pk", "pv", "pc", "pn", "pm", "ps", "sk", "sv", "sc", "sn", "sm", "ss")}

    for layer in range(depth):
        gains = norm_gains[layer]
        if layer % 2 == 0:
            e = layer // 2
            w = w_in_even[e]
            w = jnp.concatenate([w[:, :4 * na], w[:, 4 * na + 2 * H_A:], w[:, 4 * na:4 * na + 2 * H_A],
                                 jnp.zeros((D, LANES - 2 * H_A), f32)], axis=1).astype(bf16)
            brow = jnp.pad(b_gate_even[e], (0, LANES - 2 * H_A)).reshape(1, LANES)
            bcol = b_gate_even[e].reshape(2 * H_A, 1)
            gh = g_head_even[e].reshape(1, -1)
            w_out = w_out_even[e].astype(bf16)

            qa, ka, va, oa, gt, kb, vb, qb16, kb16, vb16 = _in_proj(hp, gains[0:1], w, even_outs, even_dt, tm_in)
            r3 = lambda a: a.reshape(B, Lp, a.shape[-1])
            g3 = r3(gt)
            ya, C, n, m = _mlstm(r3(qa), r3(ka), r3(va), r3(oa), g3, jnp.swapaxes(g3[:, :, :2 * H_A], 1, 2),
                                 brow, bcol, jnp.zeros((B, H_A, DK_A, DV_A), f32), jnp.zeros((B, H_A, DK_A), f32),
                                 jnp.zeros((B, H_A, 1), f32), gh, meta_pad, Lp)
            hb = _sb_prompt(r3(qb16), r3(kb16), r3(vb16), b_sb_even[e], meta_pad)
            hp = _post([ya.reshape(B * Lp, -1), hb.reshape(B * Lp, -1)], hp, w_out, gains,
                       w_up[layer].astype(bf16), w_down[layer].astype(bf16), tm_p, 512)
            outs["pk"].append(kb.reshape(B, Lp, H_B, DH_B)[:, meta_pad:])
            outs["pv"].append(vb.reshape(B, Lp, H_B, DH_B)[:, meta_pad:])
            outs["pc"].append(C)
            outs["pn"].append(n)
            outs["pm"].append(m.reshape(B, H_A))

            qa, ka, va, oa, gt, kb, vb, qb16, kb16, vb16 = _in_proj(hs, gains[0:1], w, even_outs, even_dt, tm_s)
            s3 = lambda a: a.reshape(DB, T, a.shape[-1])
            pc = lambda a: _pad_rows(s3(a), CHUNK_A)
            g3 = pc(gt)
            ya, C, n, m = _mlstm(pc(qa), pc(ka), pc(va), pc(oa), g3, jnp.swapaxes(g3[:, :, :2 * H_A], 1, 2),
                                 brow, bcol, state_mlstm_c[e], state_mlstm_n[e],
                                 state_mlstm_m[e].reshape(DB, H_A, 1), gh, 0, T)
            new = lambda a: _pad_rows(a.reshape(DB, T * H_B, DH_B), LANES)
            head = jnp.arange(LANES) % H_B
            bias_rows = jnp.where(head[:T * H_B, None] == head[None, :], b_sb_even[e][head][None, :], -jnp.inf)
            hb = _sb_sample(qb16.astype(f32).reshape(DB, T * H_B, DH_B), new(kb), new(vb), bias_rows,
                            cache_sb_k, cache_sb_v, page_table, e)
            hs = _post([ya[:, :T].reshape(DB * T, -1), hb.reshape(DB * T, -1)], hs, w_out, gains,
                       w_up[layer].astype(bf16), w_down[layer].astype(bf16), tm_s, 512)
            outs["sk"].append(kb.reshape(DB, T, H_B, DH_B))
            outs["sv"].append(vb.reshape(DB, T, H_B, DH_B))
            outs["sc"].append(C)
            outs["sn"].append(n)
            outs["sm"].append(m.reshape(DB, H_A))
        else:
            o = layer // 2
            w = jnp.concatenate([w_in_odd[o], jnp.zeros((D, LANES - GATE_RANK), f32)], axis=1).astype(bf16)
            wg = jnp.concatenate([w_gate_up_odd[o], jnp.zeros((LANES - GATE_RANK, nk), f32)], axis=0).astype(bf16)
            bg = b_gate_up_odd[o].reshape(1, nk)
            gh = g_head_odd[o].reshape(1, -1)
            w_out = w_out_odd[o].astype(bf16)

            q, k, v, r, gd = _in_proj(hp, gains[0:1], w, odd_outs, odd_dt, tm_in)
            r3 = lambda a: a.reshape(B, Lp, a.shape[-1])
            y, S = _gla(r3(q), r3(k), r3(v), r3(r), r3(gd), wg, bg, jnp.zeros((B, H_C, DK_C, DV_C), f32), gh,
                        meta_pad, Lp)
            hp = _post([y.reshape(B * Lp, -1)], hp, w_out, gains,
                       w_up[layer].astype(bf16), w_down[layer].astype(bf16), tm_p, 512)
            outs["ps"].append(S)

            q, k, v, r, gd = _in_proj(hs, gains[0:1], w, odd_outs, odd_dt, tm_s)
            pc = lambda a: _pad_rows(a.reshape(DB, T, a.shape[-1]), CHUNK_C)
            y, S = _gla(pc(q), pc(k), pc(v), pc(r), pc(gd), wg, bg, state_gla_s[o], gh, 0, T)
            hs = _post([y[:, :T].reshape(DB * T, -1)], hs, w_out, gains,
                       w_up[layer].astype(bf16), w_down[layer].astype(bf16), tm_s, 512)
            outs["ss"].append(S)

    y_prompt = hp.reshape(B, Lp, D)[:, meta_pad + n_meta:]
    y_sample = hs.reshape(DB, T, D)
    st = jnp.stack
    return (y_prompt, y_sample, st(outs["pk"]), st(outs["pv"]), st(outs["pc"]), st(outs["pn"]), st(outs["pm"]),
            st(outs["ps"]), st(outs["sk"]), st(outs["sv"]), st(outs["sc"]), st(outs["sn"]), st(outs["sm"]),
            st(outs["ss"]))
```

```python
import functools

import jax
import jax.numpy as jnp
import numpy as np
from jax import lax
from jax.experimental import pallas as pl
from jax.experimental.pallas import tpu as pltpu

f32 = jnp.float32
bf16 = jnp.bfloat16

BLOCK = 128
H_A, DK_A, DV_A = 4, 128, 128
H_B, DH_B = 8, 64
H_C, DK_C, DV_C = 4, 128, 256
GATE_RANK = 16
GATE_TAU = 16.0
CHUNK_A = 128
CHUNK_C = 64
EPS = 1e-6

LANES = 128
SUBLANES = 8
VMEM_LIMIT = 48 * 1024 * 1024

_NT = (((1,), (1,)), ((), ()))
_TN = (((0,), (0,)), ((), ()))


def _rms(x, g):
    return x * lax.rsqrt(jnp.mean(x * x, axis=-1, keepdims=True) + EPS) * g


def _softplus(z):
    return jnp.maximum(z, 0.0) + jnp.log1p(jnp.exp(-jnp.abs(z)))


def _log_sigmoid(z):
    return jnp.minimum(z, 0.0) - jnp.log1p(jnp.exp(-jnp.abs(z)))


def _split3(x):
    hi = x.astype(bf16)
    r = x - hi.astype(f32)
    mid = r.astype(bf16)
    lo = (r - mid.astype(f32)).astype(bf16)
    return hi, mid, lo


def _dot01_left(m01, x):
    return sum(jnp.dot(m01, p, preferred_element_type=f32) for p in _split3(x))


def _dot01_right(x, m01):
    return sum(jnp.dot(p, m01, preferred_element_type=f32) for p in _split3(x))


def _params(*sem):
    return pltpu.CompilerParams(dimension_semantics=sem, vmem_limit_bytes=VMEM_LIMIT)


def _in_proj_kernel(x_ref, g_ref, w_ref, *o_refs, outs):
    xn = _rms(x_ref[...], g_ref[...]).astype(bf16)
    cache = {}
    for o_ref, (off, n, scale) in zip(o_refs, outs):
        if (off, n) not in cache:
            cache[(off, n)] = jnp.dot(xn, w_ref[:, off:off + n], preferred_element_type=f32)
        y = cache[(off, n)]
        if scale != 1.0:
            y = y * scale
        o_ref[...] = y.astype(o_ref.dtype)


def _in_proj(x, g, w, outs, dtypes, tm):
    m, d = x.shape
    n_tot = w.shape[1]
    return pl.pallas_call(
        functools.partial(_in_proj_kernel, outs=outs),
        grid=(m // tm,),
        in_specs=[pl.BlockSpec((tm, d), lambda i: (i, 0)),
                  pl.BlockSpec((1, d), lambda i: (0, 0)),
                  pl.BlockSpec((d, n_tot), lambda i: (0, 0))],
        out_specs=[pl.BlockSpec((tm, n), lambda i: (i, 0)) for (_, n, _) in outs],
        out_shape=[jax.ShapeDtypeStruct((m, n), dt) for (_, n, _), dt in zip(outs, dtypes)],
        compiler_params=_params("parallel"),
        name="in_proj",
    )(x, g, w)


def _post_kernel(*refs, n_a, nf):
    a_refs = refs[:n_a]
    h_ref, wo_ref, g_ref, wu_ref, wd_ref, out_ref, h1_sc, xn_sc, acc_sc = refs[n_a:]
    f = pl.program_id(1)

    @pl.when(f == 0)
    def _():
        if n_a > 1:
            a = jnp.concatenate([r[...] for r in a_refs], axis=-1)
        else:
            a = a_refs[0][...]
        mix = jnp.dot(a, wo_ref[...], preferred_element_type=f32)
        h1 = h_ref[...] + _rms(mix, g_ref[1:2, :])
        h1_sc[...] = h1
        xn_sc[...] = _rms(h1, g_ref[2:3, :]).astype(bf16)
        acc_sc[...] = jnp.zeros_like(acc_sc)

    u = jnp.dot(xn_sc[...], wu_ref[...], preferred_element_type=f32)
    u = jnp.maximum(u, 0.0)
    acc_sc[...] += jnp.dot((u * u).astype(bf16), wd_ref[...], preferred_element_type=f32)

    @pl.when(f == nf - 1)
    def _():
        out_ref[...] = h1_sc[...] + _rms(acc_sc[...], g_ref[3:4, :])


def _post(a_list, h, w_out, gains, w_up, w_down, tm, tf):
    m, d = h.shape
    dff = w_up.shape[1]
    nf = dff // tf
    n_a = len(a_list)
    return pl.pallas_call(
        functools.partial(_post_kernel, n_a=n_a, nf=nf),
        grid=(m // tm, nf),
        in_specs=[pl.BlockSpec((tm, a.shape[1]), lambda i, f: (i, 0)) for a in a_list] + [
            pl.BlockSpec((tm, d), lambda i, f: (i, 0)),
            pl.BlockSpec(w_out.shape, lambda i, f: (0, 0)),
            pl.BlockSpec(gains.shape, lambda i, f: (0, 0)),
            pl.BlockSpec((d, tf), lambda i, f: (0, f)),
            pl.BlockSpec((tf, d), lambda i, f: (f, 0))],
        out_specs=pl.BlockSpec((tm, d), lambda i, f: (i, 0)),
        out_shape=jax.ShapeDtypeStruct((m, d), f32),
        scratch_shapes=[pltpu.VMEM((tm, d), f32), pltpu.VMEM((tm, d), bf16), pltpu.VMEM((tm, d), f32)],
        compiler_params=_params("parallel", "arbitrary"),
        name="post",
    )(*a_list, h, w_out, gains, w_up, w_down)


def _mlstm_kernel(q_ref, k_ref, v_ref, o_ref, g_ref, gt_ref, brow_ref, bcol_ref, c0_ref, n0_ref, m0_ref, gh_ref,
                  y_ref, C_ref, n_ref, m_ref, *, c, lo, hi):
    j = pl.program_id(1)

    @pl.when(j == 0)
    def _():
        C_ref[...] = c0_ref[...]
        n_ref[...] = n0_ref[...]
        m_ref[...] = m0_ref[...]

    row = lax.broadcasted_iota(jnp.int32, (c, c), 0)
    col = lax.broadcasted_iota(jnp.int32, (c, c), 1)
    causal = row >= col
    tril = jnp.where(causal, 1.0, 0.0).astype(bf16)
    triu = jnp.where(row <= col, 1.0, 0.0).astype(bf16)
    pos_c = j * c + lax.broadcasted_iota(jnp.int32, (c, 1), 0)
    valid_c = (pos_c >= lo) & (pos_c < hi)
    pos_r = j * c + lax.broadcasted_iota(jnp.int32, (1, c), 1)
    valid_r = (pos_r >= lo) & (pos_r < hi)

    gc = g_ref[0] + brow_ref[...]
    li_c = jnp.where(valid_c, gc, -jnp.inf)
    lf_c = jnp.where(valid_c, _log_sigmoid(gc), 0.0)
    b_c = _dot01_left(tril, lf_c)
    gr = gt_ref[0] + bcol_ref[...]
    li_r = jnp.where(valid_r, gr, -jnp.inf)
    lf_r = jnp.where(valid_r, _log_sigmoid(gr), 0.0)
    b_r = _dot01_right(lf_r, triu)

    heads = range(H_A)
    sl = [slice(h * DK_A, (h + 1) * DK_A) for h in heads]
    q = [q_ref[0, :, sl[h]] for h in heads]
    k = [k_ref[0, :, sl[h]] * (DK_A ** -0.5) for h in heads]
    qb = [q[h].astype(bf16) for h in heads]
    kb = [k[h].astype(bf16) for h in heads]
    vb = [v_ref[0, :, sl[h]].astype(bf16) for h in heads]
    bc = [b_c[:, H_A + h:H_A + h + 1] for h in heads]
    ic = [li_c[:, h:h + 1] for h in heads]
    br = [b_r[H_A + h:H_A + h + 1, :] for h in heads]
    ir = [li_r[h:h + 1, :] for h in heads]
    m = [m_ref[0, h:h + 1, :] for h in heads]
    C = [C_ref[0, h] for h in heads]
    n = [n_ref[0, h:h + 1, :] for h in heads]

    qk = [lax.dot_general(qb[h], kb[h], _NT, preferred_element_type=f32) for h in heads]
    qc = [jnp.dot(qb[h], C[h].astype(bf16), preferred_element_type=f32) for h in heads]
    d_log = [jnp.where(causal, bc[h] - br[h] + ir[h], -jnp.inf) for h in heads]
    inter = [bc[h] + m[h] for h in heads]
    m_t = [jnp.maximum(inter[h], jnp.max(d_log[h], axis=1, keepdims=True)) for h in heads]
    s = [qk[h] * jnp.exp(d_log[h] - m_t[h]) for h in heads]
    sv = [jnp.dot(s[h].astype(bf16), vb[h], preferred_element_type=f32) for h in heads]
    w_i = [jnp.exp(inter[h] - m_t[h]) for h in heads]
    den = [w_i[h] * jnp.sum(q[h] * n[h], axis=1, keepdims=True) + jnp.sum(s[h], axis=1, keepdims=True)
           for h in heads]

    b_last = [bc[h][c - 1:c, :] for h in heads]
    gcol = [b_last[h] - bc[h] + ic[h] for h in heads]
    m_new = [jnp.maximum(b_last[h] + m[h], jnp.max(gcol[h], axis=0, keepdims=True)) for h in heads]
    w_d = [jnp.exp(b_last[h] + m[h] - m_new[h]) for h in heads]
    kw = [k[h] * jnp.exp(gcol[h] - m_new[h]) for h in heads]
    kv = [lax.dot_general(kw[h].astype(bf16), vb[h], _TN, preferred_element_type=f32) for h in heads]
    for h in heads:
        C_ref[0, h] = w_d[h] * C[h] + kv[h]
        n_ref[0, h:h + 1, :] = w_d[h] * n[h] + jnp.sum(kw[h], axis=0, keepdims=True)
        m_ref[0, h:h + 1, :] = m_new[h]

    for h in heads:
        hh = (w_i[h] * qc[h] + sv[h]) / jnp.maximum(jnp.abs(den[h]), jnp.exp(-m_t[h]))
        y = hh * lax.rsqrt(jnp.mean(hh * hh, axis=1, keepdims=True) + EPS) * gh_ref[:, sl[h]]
        y_ref[0, :, sl[h]] = (y * jax.nn.sigmoid(o_ref[0, :, sl[h]])).astype(y_ref.dtype)


def _mlstm(q, k, v, o, g, gt, brow, bcol, c0, n0, m0, gh, lo, hi):
    nb, length, _ = q.shape
    c = CHUNK_A
    wide = pl.BlockSpec((1, c, H_A * DK_A), lambda b, j: (b, j, 0))
    st = lambda shape: pl.BlockSpec((1,) + shape, lambda b, j: (b,) + (0,) * len(shape))
    const = lambda a: pl.BlockSpec(a.shape, lambda b, j: (0,) * a.ndim)
    return pl.pallas_call(
        functools.partial(_mlstm_kernel, c=c, lo=lo, hi=hi),
        grid=(nb, length // c),
        in_specs=[wide, wide, wide, wide,
                  pl.BlockSpec((1, c, LANES), lambda b, j: (b, j, 0)),
                  pl.BlockSpec((1, 2 * H_A, c), lambda b, j: (b, 0, j)),
                  const(brow), const(bcol),
                  st((H_A, DK_A, DV_A)), st((H_A, DK_A)), st((H_A, 1)), const(gh)],
        out_specs=[wide, st((H_A, DK_A, DV_A)), st((H_A, DK_A)), st((H_A, 1))],
        out_shape=[jax.ShapeDtypeStruct((nb, length, H_A * DV_A), bf16),
                   jax.ShapeDtypeStruct((nb, H_A, DK_A, DV_A), f32),
                   jax.ShapeDtypeStruct((nb, H_A, DK_A), f32),
                   jax.ShapeDtypeStruct((nb, H_A, 1), f32)],
        compiler_params=_params("parallel", "arbitrary"),
        name="mlstm",
    )(q, k, v, o, g, gt, brow, bcol, c0, n0, m0, gh)


SB_HEADS = 4


def _neg_cumsum_table(n):
    j = np.arange(n)[:, None]
    s = np.arange(n)[None, :]
    half = np.concatenate([(j > s).astype(np.float32), np.ones((n, n), np.float32)], axis=1)
    return jnp.asarray(-np.concatenate([half, half], axis=0), bf16)


def _sb_prompt_kernel(bias_ref, q_ref, k_ref, v_ref, tab_ref, o_ref, carry_ref, acc_ref, *, tq, lo):
    hg = pl.program_id(1)
    i = pl.program_id(2)
    sub = tab_ref.shape[0] // 2
    lw = SB_HEADS * DH_B
    lane = lax.broadcasted_iota(jnp.int32, (tq, lw), 1)
    qg = q_ref[0]
    qm = [jnp.where(lane // DH_B == e, qg, jnp.zeros_like(qg)) for e in range(SB_HEADS)]
    bias = [bias_ref[SB_HEADS * hg + e] for e in range(SB_HEADS)]
    heads = range(SB_HEADS)
    carry_ref[...] = jnp.zeros_like(carry_ref)
    acc_ref[...] = jnp.zeros_like(acc_ref)

    def tile(tau, key_bias):
        s0 = pl.multiple_of(tau * tq, tq)
        kt = k_ref[0, pl.ds(s0, tq), :]
        vt = v_ref[0, pl.ds(s0, tq), :]
        z = [lax.dot_general(qm[e], kt, _NT, preferred_element_type=f32) + key_bias[e] for e in heads]
        sp = [jnp.maximum(z[e], 0.0) + jnp.log(1.0 + jnp.exp(-jnp.abs(z[e]))) for e in heads]
        hi = [sp[e].astype(bf16) for e in heads]
        lo_ = [(sp[e] - hi[e].astype(f32)).astype(bf16) for e in heads]
        r = [[jnp.dot(jnp.concatenate([hi[e][:, j * sub:(j + 1) * sub], lo_[e][:, j * sub:(j + 1) * sub]], axis=1),
                      tab_ref[...], preferred_element_type=f32) for j in range(tq // sub)] for e in heads]
        t = [z[e] - sp[e] for e in heads]
        for e in heads:
            c = carry_ref[e]
            after = [None] * (tq // sub)
            for j in reversed(range(tq // sub)):
                after[j] = r[e][j][:, :sub] + c
                c = c + r[e][j][:, sub:]
            carry_ref[e] = c
            t[e] = t[e] + jnp.concatenate(after, axis=1)
        w = [jnp.exp(t[e]).astype(bf16) for e in heads]
        for e in heads:
            acc_ref[e] += jnp.dot(w[e], vt, preferred_element_type=f32)

    row = lax.broadcasted_iota(jnp.int32, (tq, tq), 0)
    col = lax.broadcasted_iota(jnp.int32, (tq, tq), 1)
    readable = (col < row) & (i * tq + col >= lo)
    tile(i, [jnp.where(readable, b, -jnp.inf) for b in bias])

    def body(n, carry):
        tau = i - 1 - n
        spos = tau * tq + lax.broadcasted_iota(jnp.int32, (1, tq), 1)
        tile(tau, [jnp.where(spos >= lo, b, -jnp.inf) for b in bias])
        return carry

    lax.fori_loop(0, i, body, 0)
    out = acc_ref[0]
    for e in range(1, SB_HEADS):
        out = jnp.where(lane // DH_B == e, acc_ref[e], out)
    o_ref[0] = out.astype(o_ref.dtype)


def _sb_prompt(q, k, v, bias, lo):
    nb, length, width = q.shape
    nblk = length // BLOCK
    tq = BLOCK * max(d for d in (1, 2, 3, 4) if nblk % d == 0)
    lw = SB_HEADS * DH_B
    blk = pl.BlockSpec((1, tq, lw), lambda b, hg, i: (b, i, hg))
    full = pl.BlockSpec((1, length, lw), lambda b, hg, i: (b, 0, hg))
    tab = _neg_cumsum_table(BLOCK)
    return pl.pallas_call(
        functools.partial(_sb_prompt_kernel, tq=tq, lo=lo),
        grid=(nb, width // lw, length // tq),
        in_specs=[pl.BlockSpec(memory_space=pltpu.SMEM), blk, full, full,
                  pl.BlockSpec(tab.shape, lambda b, hg, i: (0, 0))],
        out_specs=blk,
        out_shape=jax.ShapeDtypeStruct((nb, length, width), bf16),
        scratch_shapes=[pltpu.VMEM((SB_HEADS, tq, BLOCK), f32), pltpu.VMEM((SB_HEADS, tq, lw), f32)],
        compiler_params=_params("parallel", "parallel", "arbitrary"),
        name="sb_prompt",
    )(bias, q, k, v, tab)


QH = 32


def _sb_sample_kernel(pt_ref, q_ref, kn_ref, vn_ref, bias_ref, tab_ref, *rest, n_pg, n_grp, n_tok):
    k_refs = rest[:n_pg]
    v_refs = rest[n_pg:2 * n_pg]
    o_ref, carry_sc, acc_sc = rest[2 * n_pg:]
    g = pl.program_id(1)
    width = H_B * DH_B
    psz = tab_ref.shape[0] // 2
    row_head = lax.broadcasted_iota(jnp.int32, (QH, width), 0) % H_B
    lane_head = lax.broadcasted_iota(jnp.int32, (QH, width), 1) // DH_B
    q = q_ref[0]
    qrows = jnp.concatenate([jnp.broadcast_to(q[t:t + 1, :], (H_B, width)) for t in range(n_tok)], axis=0)
    qcat = jnp.where(lane_head == row_head, qrows, jnp.zeros_like(qrows)).astype(bf16)

    def weights(zs):
        n = len(zs)
        z = jnp.concatenate(zs, axis=0) if n > 1 else zs[0]
        sp = jnp.maximum(z, 0.0) + jnp.log(1.0 + jnp.exp(-jnp.abs(z)))
        hi = sp.astype(bf16)
        lo_ = (sp - hi.astype(f32)).astype(bf16)
        r = jnp.dot(jnp.concatenate([hi, lo_], axis=1), tab_ref[...], preferred_element_type=f32)
        c = carry_sc[...]
        after = [None] * n
        for p in reversed(range(n)):
            after[p] = r[p * QH:(p + 1) * QH, :psz] + c
            c = c + r[p * QH:(p + 1) * QH, psz:]
        carry_sc[...] = c
        w = jnp.exp(z - sp + (jnp.concatenate(after, axis=0) if n > 1 else after[0])).astype(bf16)
        return [w[p * QH:(p + 1) * QH, :] for p in range(n)]

    @pl.when(g == 0)
    def _():
        carry_sc[...] = jnp.zeros_like(carry_sc)
        pad = jnp.zeros((psz - kn_ref.shape[1], width), f32)
        kn = jnp.concatenate([kn_ref[0], pad], axis=0).astype(bf16)
        vn = jnp.concatenate([vn_ref[0], pad], axis=0).astype(bf16)
        j = lax.broadcasted_iota(jnp.int32, (QH, psz), 1)
        t = lax.broadcasted_iota(jnp.int32, (QH, psz), 0) // H_B
        z = lax.dot_general(qcat, kn, _NT, preferred_element_type=f32) + jnp.where(j < t, bias_ref[...], -jnp.inf)
        acc_sc[...] = jnp.dot(weights([z])[0], vn, preferred_element_type=f32)

    ws = weights([jnp.dot(qcat, k_refs[p][0, 0].astype(bf16), preferred_element_type=f32) + bias_ref[...]
                  for p in range(n_pg)])
    acc = acc_sc[...]
    for p in range(n_pg):
        acc = acc + lax.dot_general(ws[p], v_refs[p][0, 0].astype(bf16), _NT, preferred_element_type=f32)
    acc_sc[...] = acc

    @pl.when(g == n_grp - 1)
    def _():
        own = jnp.where(lane_head == row_head, acc_sc[...], 0.0)
        rows = [jnp.sum(own[t * H_B:(t + 1) * H_B, :], axis=0, keepdims=True) for t in range(n_tok)]
        rows.append(jnp.zeros((o_ref.shape[1] - n_tok, width), f32))
        o_ref[0] = jnp.concatenate(rows, axis=0).astype(o_ref.dtype)


def _sb_sample(q, kn, vn, bias, cache_k, cache_v, page_table, layer, n_tok, n_pg=8):
    db, rows, width = q.shape
    n_pages = page_table.shape[1]
    n_layers, n_pool, psz = cache_k.shape[:3]
    n_grp = n_pages // n_pg
    assert n_tok * H_B == QH and n_pages % n_pg == 0
    tab = _neg_cumsum_table(psz)
    cache_k = jnp.transpose(cache_k, (0, 1, 3, 4, 2)).reshape(n_layers, n_pool, width, psz)
    cache_v = jnp.transpose(cache_v, (0, 1, 3, 4, 2)).reshape(n_layers, n_pool, width, psz)

    def page_spec(p):
        return pl.BlockSpec((1, 1, width, psz),
                            lambda b, g, pt: (layer, pt[b, (n_grp - 1 - g) * n_pg + p], 0, 0))

    small = pl.BlockSpec((1, rows, width), lambda b, g, pt: (b, 0, 0))
    grid_spec = pltpu.PrefetchScalarGridSpec(
        num_scalar_prefetch=1,
        grid=(db, n_grp),
        in_specs=[small, small, small,
                  pl.BlockSpec(bias.shape, lambda b, g, pt: (0, 0)),
                  pl.BlockSpec(tab.shape, lambda b, g, pt: (0, 0))]
        + [page_spec(p) for p in range(n_pg)] * 2,
        out_specs=small,
        scratch_shapes=[pltpu.VMEM((QH, psz), f32), pltpu.VMEM((QH, width), f32)])
    return pl.pallas_call(
        functools.partial(_sb_sample_kernel, n_pg=n_pg, n_grp=n_grp, n_tok=n_tok),
        grid_spec=grid_spec,
        out_shape=jax.ShapeDtypeStruct((db, rows, width), bf16),
        compiler_params=_params("parallel", "arbitrary"),
        name="sb_sample",
    )(page_table, q, kn, vn, bias, tab, *([cache_k] * n_pg), *([cache_v] * n_pg))


def _gla_tables(c):
    t = np.arange(c)[:, None]
    j = np.arange(c)[None, :]
    mats = [(j <= t), (j > t)]
    s = 1
    while s < c:
        bound = (t // (2 * s)) * (2 * s) + s - 1
        right = (t // s) % 2 == 1
        mats.append(np.where(right, (j > bound) & (j <= t), (j > t) & (j <= bound)))
        s *= 2
    return jnp.asarray(np.concatenate(mats, axis=0).astype(np.float32), bf16)


def _gla_kernel(q_ref, k_ref, v_ref, r_ref, gd_ref, wg_ref, bg_ref, e_ref, s0_ref, gh_ref,
                y_ref, S_ref, *, c, lo, hi):
    j = pl.program_id(1)

    @pl.when(j == 0)
    def _():
        S_ref[...] = s0_ref[...]

    pos = j * c + lax.broadcasted_iota(jnp.int32, (c, 1), 0)
    valid = (pos >= lo) & (pos < hi)
    rowc = lax.broadcasted_iota(jnp.int32, (c, 1), 0)
    row = lax.broadcasted_iota(jnp.int32, (c, c), 0)
    col = lax.broadcasted_iota(jnp.int32, (c, c), 1)
    lg = _log_sigmoid(jnp.dot(gd_ref[0].astype(bf16), wg_ref[...], preferred_element_type=f32)
                      + bg_ref[...]) / GATE_TAU
    lg = jnp.where(valid, lg, 0.0)
    n_lvl = e_ref.shape[0] // c - 2

    heads = range(H_C)
    sk = [slice(h * DK_C, (h + 1) * DK_C) for h in heads]
    sv = [slice(h * DV_C, (h + 1) * DV_C) for h in heads]
    q = [q_ref[0, :, sk[h]] * (DK_C ** -0.5) for h in heads]
    k = [jnp.where(valid, k_ref[0, :, sk[h]], 0.0) for h in heads]
    vb = [v_ref[0, :, sv[h]].astype(bf16) for h in heads]
    x = [_dot01_left(e_ref[...], lg[:, sk[h]]) for h in heads]
    fx = [jnp.exp(x[h]) for h in heads]
    S = [S_ref[0, h] for h in heads]
    o = [jnp.dot((q[h] * fx[h][0:c, :]).astype(bf16), S[h].astype(bf16), preferred_element_type=f32) for h in heads]
    kv = [lax.dot_general((k[h] * fx[h][c:2 * c, :]).astype(bf16), vb[h], _TN, preferred_element_type=f32)
          for h in heads]
    a = [jnp.where(row == col, jnp.sum(q[h] * k[h], axis=1, keepdims=True), 0.0) for h in heads]
    for lvl in range(n_lvl):
        s = 1 << lvl
        right = (rowc // s) % 2 == 1
        same = row // (2 * s) == col // (2 * s)
        fl = [fx[h][(2 + lvl) * c:(3 + lvl) * c, :] for h in heads]
        qt = [jnp.where(right, q[h] * fl[h], 0.0).astype(bf16) for h in heads]
        kt = [jnp.where(right, 0.0, k[h] * fl[h]).astype(bf16) for h in heads]
        p = [lax.dot_general(qt[h], kt[h], _NT, preferred_element_type=f32) for h in heads]
        a = [a[h] + jnp.where(same, p[h], 0.0) for h in heads]
    o = [o[h] + jnp.dot(a[h].astype(bf16), vb[h], preferred_element_type=f32) for h in heads]
    for h in heads:
        b_last = x[h][c - 1:c, :]
        decay = jnp.exp(jnp.transpose(jnp.broadcast_to(b_last, (DK_C, DK_C))))
        decay = jnp.concatenate([decay] * (DV_C // DK_C), axis=1)
        S_ref[0, h] = decay * S[h] + kv[h]
    for h in heads:
        y = o[h] * lax.rsqrt(jnp.mean(o[h] * o[h], axis=1, keepdims=True) + EPS) * gh_ref[:, sv[h]]
        y_ref[0, :, sv[h]] = (y * jax.nn.silu(r_ref[0, :, sv[h]])).astype(y_ref.dtype)


def _gla(q, k, v, r, gd, wg, bg, s0, gh, lo, hi):
    nb, length, _ = q.shape
    c = CHUNK_C
    e = _gla_tables(c)
    blk = lambda w: pl.BlockSpec((1, c, w), lambda b, j: (b, j, 0))
    const = lambda a: pl.BlockSpec(a.shape, lambda b, j: (0,) * a.ndim)
    st = pl.BlockSpec((1, H_C, DK_C, DV_C), lambda b, j: (b, 0, 0, 0))
    return pl.pallas_call(
        functools.partial(_gla_kernel, c=c, lo=lo, hi=hi),
        grid=(nb, length // c),
        in_specs=[blk(H_C * DK_C), blk(H_C * DK_C), blk(H_C * DV_C), blk(H_C * DV_C), blk(LANES),
                  const(wg), const(bg), const(e), st, const(gh)],
        out_specs=[blk(H_C * DV_C), st],
        out_shape=[jax.ShapeDtypeStruct((nb, length, H_C * DV_C), bf16),
                   jax.ShapeDtypeStruct((nb, H_C, DK_C, DV_C), f32)],
        compiler_params=_params("parallel", "arbitrary"),
        name="gla",
    )(q, k, v, r, gd, wg, bg, e, s0, gh)


def _pad_rows(x, rows):
    return jnp.pad(x, ((0, 0), (0, rows - x.shape[1]), (0, 0)))


def _row_tile(m):
    for tm in (512, 256, 128):
        if m % tm == 0:
            return tm
    raise ValueError(f"row count {m} is not a multiple of 128")


def kernel(x_prompt, x_sample, cache_sb_k, cache_sb_v, state_mlstm_c, state_mlstm_n, state_mlstm_m, state_gla_s,
           page_table, meta_tokens, norm_gains, w_in_even, b_gate_even, b_sb_even, g_head_even, w_out_even,
           w_in_odd, w_gate_up_odd, b_gate_up_odd, g_head_odd, w_out_odd, w_up, w_down):
    B, seq, D = x_prompt.shape
    DB, T, _ = x_sample.shape
    depth = norm_gains.shape[0]
    n_meta = meta_tokens.shape[0]
    meta_pad = BLOCK - n_meta
    Lp = meta_pad + n_meta + seq
    assert Lp % BLOCK == 0 and (DB * T) % LANES == 0 and T <= SUBLANES

    hp = jnp.concatenate([jnp.zeros((B, meta_pad, D), f32),
                          jnp.broadcast_to(meta_tokens[None], (B, n_meta, D)), x_prompt], axis=1)
    hp = hp.reshape(B * Lp, D)
    hs = x_sample.reshape(DB * T, D)
    tm_p, tm_s = _row_tile(B * Lp), _row_tile(DB * T)
    tm_in = min(tm_p, 256)

    na, nb_ = H_A * DK_A, H_B * DH_B
    nk, nv = H_C * DK_C, H_C * DV_C
    o_qa, o_ka, o_va, o_oa = 0, na, 2 * na, 3 * na
    o_qb, o_kb, o_vb, o_g = 4 * na, 4 * na + nb_, 4 * na + 2 * nb_, 4 * na + 3 * nb_
    even_outs = ((o_qa, na, 1.0), (o_ka, na, 1.0), (o_va, na, 1.0), (o_oa, na, 1.0), (o_g, LANES, 1.0),
                 (o_kb, nb_, 1.0), (o_vb, nb_, 1.0),
                 (o_qb, nb_, DH_B ** -0.5), (o_kb, nb_, 1.0), (o_vb, nb_, 1.0))
    even_dt = (f32,) * 7 + (bf16,) * 3
    odd_outs = ((0, nk, 1.0), (nk, nk, 1.0), (2 * nk, nv, 1.0), (2 * nk + nv, nv, 1.0), (2 * nk + 2 * nv, LANES, 1.0))
    odd_dt = (f32,) * 5

    outs = {n: [] for n in ("pk", "pv", "pc", "pn", "pm", "ps", "sk", "sv", "sc", "sn", "sm", "ss")}

    for layer in range(depth):
        gains = norm_gains[layer]
        if layer % 2 == 0:
            e = layer // 2
            w = w_in_even[e]
            w = jnp.concatenate([w[:, :4 * na], w[:, 4 * na + 2 * H_A:], w[:, 4 * na:4 * na + 2 * H_A],
                                 jnp.zeros((D, LANES - 2 * H_A), f32)], axis=1).astype(bf16)
            brow = jnp.pad(b_gate_even[e], (0, LANES - 2 * H_A)).reshape(1, LANES)
            bcol = b_gate_even[e].reshape(2 * H_A, 1)
            gh = g_head_even[e].reshape(1, -1)
            w_out = w_out_even[e].astype(bf16)

            qa, ka, va, oa, gt, kb, vb, qb16, kb16, vb16 = _in_proj(hp, gains[0:1], w, even_outs, even_dt, tm_in)
            r3 = lambda a: a.reshape(B, Lp, a.shape[-1])
            g3 = r3(gt)
            ya, C, n, m = _mlstm(r3(qa), r3(ka), r3(va), r3(oa), g3, jnp.swapaxes(g3[:, :, :2 * H_A], 1, 2),
                                 brow, bcol, jnp.zeros((B, H_A, DK_A, DV_A), f32), jnp.zeros((B, H_A, DK_A), f32),
                                 jnp.zeros((B, H_A, 1), f32), gh, meta_pad, Lp)
            hb = _sb_prompt(r3(qb16), r3(kb16), r3(vb16), b_sb_even[e], meta_pad)
            hp = _post([ya.reshape(B * Lp, -1), hb.reshape(B * Lp, -1)], hp, w_out, gains,
                       w_up[layer].astype(bf16), w_down[layer].astype(bf16), tm_p, 512)
            outs["pk"].append(kb.reshape(B, Lp, H_B, DH_B)[:, meta_pad:])
            outs["pv"].append(vb.reshape(B, Lp, H_B, DH_B)[:, meta_pad:])
            outs["pc"].append(C)
            outs["pn"].append(n)
            outs["pm"].append(m.reshape(B, H_A))

            qa, ka, va, oa, gt, kb, vb, qb16, kb16, vb16 = _in_proj(hs, gains[0:1], w, even_outs, even_dt, tm_s)
            s3 = lambda a: a.reshape(DB, T, a.shape[-1])
            pc = lambda a: _pad_rows(s3(a), CHUNK_A)
            g3 = pc(gt)
            ya, C, n, m = _mlstm(pc(qa), pc(ka), pc(va), pc(oa), g3, jnp.swapaxes(g3[:, :, :2 * H_A], 1, 2),
                                 brow, bcol, state_mlstm_c[e], state_mlstm_n[e],
                                 state_mlstm_m[e].reshape(DB, H_A, 1), gh, 0, T)
            p8 = lambda a: _pad_rows(s3(a), SUBLANES)
            bias_rows = jnp.broadcast_to(jnp.tile(b_sb_even[e], T)[:, None], (T * H_B, cache_sb_k.shape[2]))
            hb = _sb_sample(p8(qb16.astype(f32)), p8(kb), p8(vb), bias_rows, cache_sb_k, cache_sb_v, page_table, e, T)
            hs = _post([ya[:, :T].reshape(DB * T, -1), hb[:, :T].reshape(DB * T, -1)], hs, w_out, gains,
                       w_up[layer].astype(bf16), w_down[layer].astype(bf16), tm_s, 512)
            outs["sk"].append(kb.reshape(DB, T, H_B, DH_B))
            outs["sv"].append(vb.reshape(DB, T, H_B, DH_B))
            outs["sc"].append(C)
            outs["sn"].append(n)
            outs["sm"].append(m.reshape(DB, H_A))
        else:
            o = layer // 2
            w = jnp.concatenate([w_in_odd[o], jnp.zeros((D, LANES - GATE_RANK), f32)], axis=1).astype(bf16)
            wg = jnp.concatenate([w_gate_up_odd[o], jnp.zeros((LANES - GATE_RANK, nk), f32)], axis=0).astype(bf16)
            bg = b_gate_up_odd[o].reshape(1, nk)
            gh = g_head_odd[o].reshape(1, -1)
            w_out = w_out_odd[o].astype(bf16)

            q, k, v, r, gd = _in_proj(hp, gains[0:1], w, odd_outs, odd_dt, tm_in)
            r3 = lambda a: a.reshape(B, Lp, a.shape[-1])
            y, S = _gla(r3(q), r3(k), r3(v), r3(r), r3(gd), wg, bg, jnp.zeros((B, H_C, DK_C, DV_C), f32), gh,
                        meta_pad, Lp)
            hp = _post([y.reshape(B * Lp, -1)], hp, w_out, gains,
                       w_up[layer].astype(bf16), w_down[layer].astype(bf16), tm_p, 512)
            outs["ps"].append(S)

            q, k, v, r, gd = _in_proj(hs, gains[0:1], w, odd_outs, odd_dt, tm_s)
            pc = lambda a: _pad_rows(a.reshape(DB, T, a.shape[-1]), CHUNK_C)
            y, S = _gla(pc(q), pc(k), pc(v), pc(r), pc(gd), wg, bg, state_gla_s[o], gh, 0, T)
            hs = _post([y[:, :T].reshape(DB * T, -1)], hs, w_out, gains,
                       w_up[layer].astype(bf16), w_down[layer].astype(bf16), tm_s, 512)
            outs["ss"].append(S)

    y_prompt = hp.reshape(B, Lp, D)[:, meta_pad + n_meta:]
    y_sample = hs.reshape(DB, T, D)
    st = jnp.stack
    return (y_prompt, y_sample, st(outs["pk"]), st(outs["pv"]), st(outs["pc"]), st(outs["pn"]), st(outs["pm"]),
            st(outs["ps"]), st(outs["sk"]), st(outs["sv"]), st(outs["sc"]), st(outs["sn"]), st(outs["sm"]),
            st(outs["ss"]))
```

```python
import functools

import jax
import jax.numpy as jnp
import numpy as np
from jax import lax
from jax.experimental import pallas as pl
from jax.experimental.pallas import tpu as pltpu

f32 = jnp.float32
bf16 = jnp.bfloat16

BLOCK = 128
H_A, DK_A, DV_A = 4, 128, 128
H_B, DH_B = 8, 64
H_C, DK_C, DV_C = 4, 128, 256
GATE_RANK = 16
GATE_TAU = 16.0
CHUNK_A = 128
CHUNK_C = 64
EPS = 1e-6
LOG2E = 1.4426950408889634

LANES = 128
SUBLANES = 8
VMEM_LIMIT = 48 * 1024 * 1024

_NT = (((1,), (1,)), ((), ()))
_TN = (((0,), (0,)), ((), ()))


def _rms(x, g):
    return x * lax.rsqrt(jnp.mean(x * x, axis=-1, keepdims=True) + EPS) * g


def _softplus(z):
    return jnp.maximum(z, 0.0) + jnp.log(1.0 + jnp.exp2(jnp.abs(z) * (-LOG2E)))


def _log_sigmoid(z):
    return jnp.minimum(z, 0.0) - jnp.log1p(jnp.exp(-jnp.abs(z)))


def _split3(x):
    hi = x.astype(bf16)
    r = x - hi.astype(f32)
    mid = r.astype(bf16)
    lo = (r - mid.astype(f32)).astype(bf16)
    return hi, mid, lo


def _dot01_left(m01, x):
    return sum(jnp.dot(m01, p, preferred_element_type=f32) for p in _split3(x))


def _dot01_right(x, m01):
    return sum(jnp.dot(p, m01, preferred_element_type=f32) for p in _split3(x))


def _params(*sem):
    return pltpu.CompilerParams(dimension_semantics=sem, vmem_limit_bytes=VMEM_LIMIT)


def _in_proj_kernel(x_ref, g_ref, w_ref, *o_refs, outs):
    xn = _rms(x_ref[...], g_ref[...]).astype(bf16)
    cache = {}
    for o_ref, (off, n, scale) in zip(o_refs, outs):
        if (off, n) not in cache:
            cache[(off, n)] = jnp.dot(xn, w_ref[:, off:off + n], preferred_element_type=f32)
        y = cache[(off, n)]
        if scale != 1.0:
            y = y * scale
        o_ref[...] = y.astype(o_ref.dtype)


def _in_proj(x, g, w, outs, dtypes, tm):
    m, d = x.shape
    n_tot = w.shape[1]
    return pl.pallas_call(
        functools.partial(_in_proj_kernel, outs=outs),
        grid=(m // tm,),
        in_specs=[pl.BlockSpec((tm, d), lambda i: (i, 0)),
                  pl.BlockSpec((1, d), lambda i: (0, 0)),
                  pl.BlockSpec((d, n_tot), lambda i: (0, 0))],
        out_specs=[pl.BlockSpec((tm, n), lambda i: (i, 0)) for (_, n, _) in outs],
        out_shape=[jax.ShapeDtypeStruct((m, n), dt) for (_, n, _), dt in zip(outs, dtypes)],
        compiler_params=_params("parallel"),
        name="in_proj",
    )(x, g, w)


def _post_kernel(*refs, n_a, nf):
    a_refs = refs[:n_a]
    h_ref, wo_ref, g_ref, wu_ref, wd_ref, out_ref, h1_sc, xn_sc, acc_sc = refs[n_a:]
    f = pl.program_id(1)

    @pl.when(f == 0)
    def _():
        if n_a > 1:
            a = jnp.concatenate([r[...] for r in a_refs], axis=-1)
        else:
            a = a_refs[0][...]
        mix = jnp.dot(a, wo_ref[...], preferred_element_type=f32)
        h1 = h_ref[...] + _rms(mix, g_ref[1:2, :])
        h1_sc[...] = h1
        xn_sc[...] = _rms(h1, g_ref[2:3, :]).astype(bf16)
        acc_sc[...] = jnp.zeros_like(acc_sc)

    u = jnp.dot(xn_sc[...], wu_ref[...], preferred_element_type=f32)
    u = jnp.maximum(u, 0.0)
    acc_sc[...] += jnp.dot((u * u).astype(bf16), wd_ref[...], preferred_element_type=f32)

    @pl.when(f == nf - 1)
    def _():
        out_ref[...] = h1_sc[...] + _rms(acc_sc[...], g_ref[3:4, :])


def _post(a_list, h, w_out, gains, w_up, w_down, tm, tf):
    m, d = h.shape
    dff = w_up.shape[1]
    nf = dff // tf
    n_a = len(a_list)
    return pl.pallas_call(
        functools.partial(_post_kernel, n_a=n_a, nf=nf),
        grid=(m // tm, nf),
        in_specs=[pl.BlockSpec((tm, a.shape[1]), lambda i, f: (i, 0)) for a in a_list] + [
            pl.BlockSpec((tm, d), lambda i, f: (i, 0)),
            pl.BlockSpec(w_out.shape, lambda i, f: (0, 0)),
            pl.BlockSpec(gains.shape, lambda i, f: (0, 0)),
            pl.BlockSpec((d, tf), lambda i, f: (0, f)),
            pl.BlockSpec((tf, d), lambda i, f: (f, 0))],
        out_specs=pl.BlockSpec((tm, d), lambda i, f: (i, 0)),
        out_shape=jax.ShapeDtypeStruct((m, d), f32),
        scratch_shapes=[pltpu.VMEM((tm, d), f32), pltpu.VMEM((tm, d), bf16), pltpu.VMEM((tm, d), f32)],
        compiler_params=_params("parallel", "arbitrary"),
        name="post",
    )(*a_list, h, w_out, gains, w_up, w_down)


def _mlstm_kernel(q_ref, k_ref, v_ref, o_ref, g_ref, gt_ref, brow_ref, bcol_ref, c0_ref, n0_ref, m0_ref, gh_ref,
                  y_ref, C_ref, n_ref, m_ref, *, c, lo, hi):
    j = pl.program_id(1)

    @pl.when(j == 0)
    def _():
        C_ref[...] = c0_ref[...]
        n_ref[...] = n0_ref[...]
        m_ref[...] = m0_ref[...]

    row = lax.broadcasted_iota(jnp.int32, (c, c), 0)
    col = lax.broadcasted_iota(jnp.int32, (c, c), 1)
    causal = row >= col
    tril = jnp.where(causal, 1.0, 0.0).astype(bf16)
    triu = jnp.where(row <= col, 1.0, 0.0).astype(bf16)
    pos_c = j * c + lax.broadcasted_iota(jnp.int32, (c, 1), 0)
    valid_c = (pos_c >= lo) & (pos_c < hi)
    pos_r = j * c + lax.broadcasted_iota(jnp.int32, (1, c), 1)
    valid_r = (pos_r >= lo) & (pos_r < hi)

    gc = g_ref[0] + brow_ref[...]
    li_c = jnp.where(valid_c, gc, -jnp.inf)
    lf_c = jnp.where(valid_c, _log_sigmoid(gc), 0.0)
    b_c = _dot01_left(tril, lf_c)
    gr = gt_ref[0] + bcol_ref[...]
    li_r = jnp.where(valid_r, gr, -jnp.inf)
    lf_r = jnp.where(valid_r, _log_sigmoid(gr), 0.0)
    b_r = _dot01_right(lf_r, triu)

    heads = range(H_A)
    sl = [slice(h * DK_A, (h + 1) * DK_A) for h in heads]
    q = [q_ref[0, :, sl[h]] for h in heads]
    k = [k_ref[0, :, sl[h]] * (DK_A ** -0.5) for h in heads]
    qb = [q[h].astype(bf16) for h in heads]
    kb = [k[h].astype(bf16) for h in heads]
    vb = [v_ref[0, :, sl[h]].astype(bf16) for h in heads]
    bc = [b_c[:, H_A + h:H_A + h + 1] for h in heads]
    ic = [li_c[:, h:h + 1] for h in heads]
    br = [b_r[H_A + h:H_A + h + 1, :] for h in heads]
    ir = [li_r[h:h + 1, :] for h in heads]
    m = [m_ref[0, h:h + 1, :] for h in heads]
    C = [C_ref[0, h] for h in heads]
    n = [n_ref[0, h:h + 1, :] for h in heads]

    qk = [lax.dot_general(qb[h], kb[h], _NT, preferred_element_type=f32) for h in heads]
    qc = [jnp.dot(qb[h], C[h].astype(bf16), preferred_element_type=f32) for h in heads]
    d_log = [jnp.where(causal, bc[h] - br[h] + ir[h], -jnp.inf) for h in heads]
    inter = [bc[h] + m[h] for h in heads]
    m_t = [jnp.maximum(inter[h], jnp.max(d_log[h], axis=1, keepdims=True)) for h in heads]
    s = [qk[h] * jnp.exp(d_log[h] - m_t[h]) for h in heads]
    sv = [jnp.dot(s[h].astype(bf16), vb[h], preferred_element_type=f32) for h in heads]
    w_i = [jnp.exp(inter[h] - m_t[h]) for h in heads]
    den = [w_i[h] * jnp.sum(q[h] * n[h], axis=1, keepdims=True) + jnp.sum(s[h], axis=1, keepdims=True)
           for h in heads]

    b_last = [bc[h][c - 1:c, :] for h in heads]
    gcol = [b_last[h] - bc[h] + ic[h] for h in heads]
    m_new = [jnp.maximum(b_last[h] + m[h], jnp.max(gcol[h], axis=0, keepdims=True)) for h in heads]
    w_d = [jnp.exp(b_last[h] + m[h] - m_new[h]) for h in heads]
    kw = [k[h] * jnp.exp(gcol[h] - m_new[h]) for h in heads]
    kv = [lax.dot_general(kw[h].astype(bf16), vb[h], _TN, preferred_element_type=f32) for h in heads]
    for h in heads:
        C_ref[0, h] = w_d[h] * C[h] + kv[h]
        n_ref[0, h:h + 1, :] = w_d[h] * n[h] + jnp.sum(kw[h], axis=0, keepdims=True)
        m_ref[0, h:h + 1, :] = m_new[h]

    for h in heads:
        hh = (w_i[h] * qc[h] + sv[h]) / jnp.maximum(jnp.abs(den[h]), jnp.exp(-m_t[h]))
        y = hh * lax.rsqrt(jnp.mean(hh * hh, axis=1, keepdims=True) + EPS) * gh_ref[:, sl[h]]
        y_ref[0, :, sl[h]] = (y * jax.nn.sigmoid(o_ref[0, :, sl[h]])).astype(y_ref.dtype)


def _mlstm(q, k, v, o, g, gt, brow, bcol, c0, n0, m0, gh, lo, hi):
    nb, length, _ = q.shape
    c = CHUNK_A
    wide = pl.BlockSpec((1, c, H_A * DK_A), lambda b, j: (b, j, 0))
    st = lambda shape: pl.BlockSpec((1,) + shape, lambda b, j: (b,) + (0,) * len(shape))
    const = lambda a: pl.BlockSpec(a.shape, lambda b, j: (0,) * a.ndim)
    return pl.pallas_call(
        functools.partial(_mlstm_kernel, c=c, lo=lo, hi=hi),
        grid=(nb, length // c),
        in_specs=[wide, wide, wide, wide,
                  pl.BlockSpec((1, c, LANES), lambda b, j: (b, j, 0)),
                  pl.BlockSpec((1, 2 * H_A, c), lambda b, j: (b, 0, j)),
                  const(brow), const(bcol),
                  st((H_A, DK_A, DV_A)), st((H_A, DK_A)), st((H_A, 1)), const(gh)],
        out_specs=[wide, st((H_A, DK_A, DV_A)), st((H_A, DK_A)), st((H_A, 1))],
        out_shape=[jax.ShapeDtypeStruct((nb, length, H_A * DV_A), bf16),
                   jax.ShapeDtypeStruct((nb, H_A, DK_A, DV_A), f32),
                   jax.ShapeDtypeStruct((nb, H_A, DK_A), f32),
                   jax.ShapeDtypeStruct((nb, H_A, 1), f32)],
        compiler_params=_params("parallel", "arbitrary"),
        name="mlstm",
    )(q, k, v, o, g, gt, brow, bcol, c0, n0, m0, gh)


SB_HEADS = 4


def _neg_cumsum_table(n):
    j = np.arange(n)[:, None]
    s = np.arange(n)[None, :]
    half = np.concatenate([(j > s).astype(np.float32), np.ones((n, n), np.float32)], axis=1)
    return jnp.asarray(-np.concatenate([half, half], axis=0), bf16)


def _sb_prompt_kernel(bias_ref, q_ref, k_ref, v_ref, tab_ref, o_ref, carry_ref, acc_ref, *, tq, lo):
    hg = pl.program_id(1)
    i = pl.program_id(2)
    sub = tab_ref.shape[0] // 2
    lw = SB_HEADS * DH_B
    lane = lax.broadcasted_iota(jnp.int32, (tq, lw), 1)
    qg = q_ref[0]
    qm = [jnp.where(lane // DH_B == e, qg, jnp.zeros_like(qg)) for e in range(SB_HEADS)]
    bias = [bias_ref[SB_HEADS * hg + e] for e in range(SB_HEADS)]
    heads = range(SB_HEADS)
    carry_ref[...] = jnp.zeros_like(carry_ref)
    acc_ref[...] = jnp.zeros_like(acc_ref)

    def tile(tau, key_bias):
        s0 = pl.multiple_of(tau * tq, tq)
        kt = k_ref[0, pl.ds(s0, tq), :]
        vt = v_ref[0, pl.ds(s0, tq), :]
        z = [lax.dot_general(qm[e], kt, _NT, preferred_element_type=f32) + key_bias[e] for e in heads]
        sp = [_softplus(z[e]) for e in heads]
        hi = [sp[e].astype(bf16) for e in heads]
        lo_ = [(sp[e] - hi[e].astype(f32)).astype(bf16) for e in heads]
        r = [[jnp.dot(jnp.concatenate([hi[e][:, j * sub:(j + 1) * sub], lo_[e][:, j * sub:(j + 1) * sub]], axis=1),
                      tab_ref[...], preferred_element_type=f32) for j in range(tq // sub)] for e in heads]
        t = [z[e] - sp[e] for e in heads]
        for e in heads:
            c = carry_ref[e]
            after = [None] * (tq // sub)
            for j in reversed(range(tq // sub)):
                after[j] = r[e][j][:, :sub] + c
                c = c + r[e][j][:, sub:]
            carry_ref[e] = c
            t[e] = t[e] + jnp.concatenate(after, axis=1)
        w = [jnp.exp(t[e]).astype(bf16) for e in heads]
        for e in heads:
            acc_ref[e] += jnp.dot(w[e], vt, preferred_element_type=f32)

    row = lax.broadcasted_iota(jnp.int32, (tq, tq), 0)
    col = lax.broadcasted_iota(jnp.int32, (tq, tq), 1)
    readable = (col < row) & (i * tq + col >= lo)
    tile(i, [jnp.where(readable, b, -jnp.inf) for b in bias])

    def body(n, carry):
        tau = i - 1 - n
        spos = tau * tq + lax.broadcasted_iota(jnp.int32, (1, tq), 1)
        tile(tau, [jnp.where(spos >= lo, b, -jnp.inf) for b in bias])
        return carry

    lax.fori_loop(0, i, body, 0)
    out = acc_ref[0]
    for e in range(1, SB_HEADS):
        out = jnp.where(lane // DH_B == e, acc_ref[e], out)
    o_ref[0] = out.astype(o_ref.dtype)


def _sb_prompt(q, k, v, bias, lo):
    nb, length, width = q.shape
    nblk = length // BLOCK
    tq = BLOCK * max(d for d in (1, 2, 3, 4) if nblk % d == 0)
    lw = SB_HEADS * DH_B
    blk = pl.BlockSpec((1, tq, lw), lambda b, hg, i: (b, i, hg))
    full = pl.BlockSpec((1, length, lw), lambda b, hg, i: (b, 0, hg))
    tab = _neg_cumsum_table(BLOCK)
    return pl.pallas_call(
        functools.partial(_sb_prompt_kernel, tq=tq, lo=lo),
        grid=(nb, width // lw, length // tq),
        in_specs=[pl.BlockSpec(memory_space=pltpu.SMEM), blk, full, full,
                  pl.BlockSpec(tab.shape, lambda b, hg, i: (0, 0))],
        out_specs=blk,
        out_shape=jax.ShapeDtypeStruct((nb, length, width), bf16),
        scratch_shapes=[pltpu.VMEM((SB_HEADS, tq, BLOCK), f32), pltpu.VMEM((SB_HEADS, tq, lw), f32)],
        compiler_params=_params("parallel", "parallel", "arbitrary"),
        name="sb_prompt",
    )(bias, q, k, v, tab)


QH = 32


def _sb_sample_kernel(pt_ref, q_ref, kn_ref, vn_ref, bias_ref, tab_ref, *rest, n_pg, n_grp, n_tok):
    k_refs = rest[:n_pg]
    v_refs = rest[n_pg:2 * n_pg]
    o_ref, carry_sc, acc_sc = rest[2 * n_pg:]
    g = pl.program_id(1)
    width = H_B * DH_B
    psz = tab_ref.shape[0] // 2
    row_head = lax.broadcasted_iota(jnp.int32, (QH, width), 0) % H_B
    lane_head = lax.broadcasted_iota(jnp.int32, (QH, width), 1) // DH_B
    q = q_ref[0]
    qrows = jnp.concatenate([jnp.broadcast_to(q[t:t + 1, :], (H_B, width)) for t in range(n_tok)], axis=0)
    qcat = jnp.where(lane_head == row_head, qrows, jnp.zeros_like(qrows)).astype(bf16)

    def weights(zs):
        n = len(zs)
        z = jnp.concatenate(zs, axis=0) if n > 1 else zs[0]
        sp = _softplus(z)
        hi = sp.astype(bf16)
        lo_ = (sp - hi.astype(f32)).astype(bf16)
        r = jnp.dot(jnp.concatenate([hi, lo_], axis=1), tab_ref[...], preferred_element_type=f32)
        c = carry_sc[...]
        after = [None] * n
        for p in reversed(range(n)):
            after[p] = r[p * QH:(p + 1) * QH, :psz] + c
            c = c + r[p * QH:(p + 1) * QH, psz:]
        carry_sc[...] = c
        w = jnp.exp(z - sp + (jnp.concatenate(after, axis=0) if n > 1 else after[0])).astype(bf16)
        return [w[p * QH:(p + 1) * QH, :] for p in range(n)]

    @pl.when(g == 0)
    def _():
        carry_sc[...] = jnp.zeros_like(carry_sc)
        pad = jnp.zeros((psz - kn_ref.shape[1], width), f32)
        kn = jnp.concatenate([kn_ref[0], pad], axis=0).astype(bf16)
        vn = jnp.concatenate([vn_ref[0], pad], axis=0).astype(bf16)
        j = lax.broadcasted_iota(jnp.int32, (QH, psz), 1)
        t = lax.broadcasted_iota(jnp.int32, (QH, psz), 0) // H_B
        z = lax.dot_general(qcat, kn, _NT, preferred_element_type=f32) + jnp.where(j < t, bias_ref[...], -jnp.inf)
        acc_sc[...] = jnp.dot(weights([z])[0], vn, preferred_element_type=f32)

    ws = weights([jnp.dot(qcat, k_refs[p][0, 0].astype(bf16), preferred_element_type=f32) + bias_ref[...]
                  for p in range(n_pg)])
    acc = acc_sc[...]
    for p in range(n_pg):
        acc = acc + lax.dot_general(ws[p], v_refs[p][0, 0].astype(bf16), _NT, preferred_element_type=f32)
    acc_sc[...] = acc

    @pl.when(g == n_grp - 1)
    def _():
        own = jnp.where(lane_head == row_head, acc_sc[...], 0.0)
        rows = [jnp.sum(own[t * H_B:(t + 1) * H_B, :], axis=0, keepdims=True) for t in range(n_tok)]
        rows.append(jnp.zeros((o_ref.shape[1] - n_tok, width), f32))
        o_ref[0] = jnp.concatenate(rows, axis=0).astype(o_ref.dtype)


def _sb_sample(q, kn, vn, bias, cache_k, cache_v, page_table, layer, n_tok, n_pg=16):
    db, rows, width = q.shape
    n_pages = page_table.shape[1]
    n_layers, n_pool, psz = cache_k.shape[:3]
    n_grp = n_pages // n_pg
    assert n_tok * H_B == QH and n_pages % n_pg == 0
    tab = _neg_cumsum_table(psz)
    cache_k = jnp.transpose(cache_k, (0, 1, 3, 4, 2)).reshape(n_layers, n_pool, width, psz)
    cache_v = jnp.transpose(cache_v, (0, 1, 3, 4, 2)).reshape(n_layers, n_pool, width, psz)

    def page_spec(p):
        return pl.BlockSpec((1, 1, width, psz),
                            lambda b, g, pt: (layer, pt[b, (n_grp - 1 - g) * n_pg + p], 0, 0))

    small = pl.BlockSpec((1, rows, width), lambda b, g, pt: (b, 0, 0))
    grid_spec = pltpu.PrefetchScalarGridSpec(
        num_scalar_prefetch=1,
        grid=(db, n_grp),
        in_specs=[small, small, small,
                  pl.BlockSpec(bias.shape, lambda b, g, pt: (0, 0)),
                  pl.BlockSpec(tab.shape, lambda b, g, pt: (0, 0))]
        + [page_spec(p) for p in range(n_pg)] * 2,
        out_specs=small,
        scratch_shapes=[pltpu.VMEM((QH, psz), f32), pltpu.VMEM((QH, width), f32)])
    return pl.pallas_call(
        functools.partial(_sb_sample_kernel, n_pg=n_pg, n_grp=n_grp, n_tok=n_tok),
        grid_spec=grid_spec,
        out_shape=jax.ShapeDtypeStruct((db, rows, width), bf16),
        compiler_params=_params("parallel", "arbitrary"),
        name="sb_sample",
    )(page_table, q, kn, vn, bias, tab, *([cache_k] * n_pg), *([cache_v] * n_pg))


def _gla_tables(c):
    t = np.arange(c)[:, None]
    j = np.arange(c)[None, :]
    mats = [(j <= t), (j > t)]
    s = 1
    while s < c:
        bound = (t // (2 * s)) * (2 * s) + s - 1
        right = (t // s) % 2 == 1
        mats.append(np.where(right, (j > bound) & (j <= t), (j > t) & (j <= bound)))
        s *= 2
    return jnp.asarray(np.concatenate(mats, axis=0).astype(np.float32), bf16)


def _gla_kernel(q_ref, k_ref, v_ref, r_ref, gd_ref, wg_ref, bg_ref, e_ref, s0_ref, gh_ref,
                y_ref, S_ref, *, c, lo, hi):
    j = pl.program_id(1)

    @pl.when(j == 0)
    def _():
        S_ref[...] = s0_ref[...]

    pos = j * c + lax.broadcasted_iota(jnp.int32, (c, 1), 0)
    valid = (pos >= lo) & (pos < hi)
    rowc = lax.broadcasted_iota(jnp.int32, (c, 1), 0)
    row = lax.broadcasted_iota(jnp.int32, (c, c), 0)
    col = lax.broadcasted_iota(jnp.int32, (c, c), 1)
    lg = _log_sigmoid(jnp.dot(gd_ref[0].astype(bf16), wg_ref[...], preferred_element_type=f32)
                      + bg_ref[...]) / GATE_TAU
    lg = jnp.where(valid, lg, 0.0)
    n_lvl = e_ref.shape[0] // c - 2

    heads = range(H_C)
    sk = [slice(h * DK_C, (h + 1) * DK_C) for h in heads]
    sv = [slice(h * DV_C, (h + 1) * DV_C) for h in heads]
    q = [q_ref[0, :, sk[h]] * (DK_C ** -0.5) for h in heads]
    k = [jnp.where(valid, k_ref[0, :, sk[h]], 0.0) for h in heads]
    vb = [v_ref[0, :, sv[h]].astype(bf16) for h in heads]
    x = [_dot01_left(e_ref[...], lg[:, sk[h]]) for h in heads]
    fx = [jnp.exp(x[h]) for h in heads]
    S = [S_ref[0, h] for h in heads]
    o = [jnp.dot((q[h] * fx[h][0:c, :]).astype(bf16), S[h].astype(bf16), preferred_element_type=f32) for h in heads]
    kv = [lax.dot_general((k[h] * fx[h][c:2 * c, :]).astype(bf16), vb[h], _TN, preferred_element_type=f32)
          for h in heads]
    a = [jnp.where(row == col, jnp.sum(q[h] * k[h], axis=1, keepdims=True), 0.0) for h in heads]
    for lvl in range(n_lvl):
        s = 1 << lvl
        right = (rowc // s) % 2 == 1
        same = row // (2 * s) == col // (2 * s)
        fl = [fx[h][(2 + lvl) * c:(3 + lvl) * c, :] for h in heads]
        qt = [jnp.where(right, q[h] * fl[h], 0.0).astype(bf16) for h in heads]
        kt = [jnp.where(right, 0.0, k[h] * fl[h]).astype(bf16) for h in heads]
        p = [lax.dot_general(qt[h], kt[h], _NT, preferred_element_type=f32) for h in heads]
        a = [a[h] + jnp.where(same, p[h], 0.0) for h in heads]
    o = [o[h] + jnp.dot(a[h].astype(bf16), vb[h], preferred_element_type=f32) for h in heads]
    for h in heads:
        b_last = x[h][c - 1:c, :]
        decay = jnp.exp(jnp.transpose(jnp.broadcast_to(b_last, (DK_C, DK_C))))
        decay = jnp.concatenate([decay] * (DV_C // DK_C), axis=1)
        S_ref[0, h] = decay * S[h] + kv[h]
    for h in heads:
        y = o[h] * lax.rsqrt(jnp.mean(o[h] * o[h], axis=1, keepdims=True) + EPS) * gh_ref[:, sv[h]]
        y_ref[0, :, sv[h]] = (y * jax.nn.silu(r_ref[0, :, sv[h]])).astype(y_ref.dtype)


def _gla(q, k, v, r, gd, wg, bg, s0, gh, lo, hi):
    nb, length, _ = q.shape
    c = CHUNK_C
    e = _gla_tables(c)
    blk = lambda w: pl.BlockSpec((1, c, w), lambda b, j: (b, j, 0))
    const = lambda a: pl.BlockSpec(a.shape, lambda b, j: (0,) * a.ndim)
    st = pl.BlockSpec((1, H_C, DK_C, DV_C), lambda b, j: (b, 0, 0, 0))
    return pl.pallas_call(
        functools.partial(_gla_kernel, c=c, lo=lo, hi=hi),
        grid=(nb, length // c),
        in_specs=[blk(H_C * DK_C), blk(H_C * DK_C), blk(H_C * DV_C), blk(H_C * DV_C), blk(LANES),
                  const(wg), const(bg), const(e), st, const(gh)],
        out_specs=[blk(H_C * DV_C), st],
        out_shape=[jax.ShapeDtypeStruct((nb, length, H_C * DV_C), bf16),
                   jax.ShapeDtypeStruct((nb, H_C, DK_C, DV_C), f32)],
        compiler_params=_params("parallel", "arbitrary"),
        name="gla",
    )(q, k, v, r, gd, wg, bg, e, s0, gh)


def _pad_rows(x, rows):
    return jnp.pad(x, ((0, 0), (0, rows - x.shape[1]), (0, 0)))


POST_ROW_TILES = (768, 512, 256, 128)
POST_FF_TILE = 1024


def _row_tile(m, candidates):
    for tm in candidates:
        if m % tm == 0:
            return tm
    raise ValueError(f"row count {m} is not a multiple of {candidates[-1]}")


def kernel(x_prompt, x_sample, cache_sb_k, cache_sb_v, state_mlstm_c, state_mlstm_n, state_mlstm_m, state_gla_s,
           page_table, meta_tokens, norm_gains, w_in_even, b_gate_even, b_sb_even, g_head_even, w_out_even,
           w_in_odd, w_gate_up_odd, b_gate_up_odd, g_head_odd, w_out_odd, w_up, w_down):
    B, seq, D = x_prompt.shape
    DB, T, _ = x_sample.shape
    depth = norm_gains.shape[0]
    n_meta = meta_tokens.shape[0]
    meta_pad = BLOCK - n_meta
    Lp = meta_pad + n_meta + seq
    assert Lp % BLOCK == 0 and (DB * T) % LANES == 0 and T <= SUBLANES

    hp = jnp.concatenate([jnp.zeros((B, meta_pad, D), f32),
                          jnp.broadcast_to(meta_tokens[None], (B, n_meta, D)), x_prompt], axis=1)
    hp = hp.reshape(B * Lp, D)
    hs = x_sample.reshape(DB * T, D)
    tm_p, tm_s = _row_tile(B * Lp, POST_ROW_TILES), _row_tile(DB * T, POST_ROW_TILES)
    tm_in = _row_tile(B * Lp, (256, 128))

    na, nb_ = H_A * DK_A, H_B * DH_B
    nk, nv = H_C * DK_C, H_C * DV_C
    o_qa, o_ka, o_va, o_oa = 0, na, 2 * na, 3 * na
    o_qb, o_kb, o_vb, o_g = 4 * na, 4 * na + nb_, 4 * na + 2 * nb_, 4 * na + 3 * nb_
    even_outs = ((o_qa, na, 1.0), (o_ka, na, 1.0), (o_va, na, 1.0), (o_oa, na, 1.0), (o_g, LANES, 1.0),
                 (o_kb, nb_, 1.0), (o_vb, nb_, 1.0),
                 (o_qb, nb_, DH_B ** -0.5), (o_kb, nb_, 1.0), (o_vb, nb_, 1.0))
    even_dt = (f32,) * 7 + (bf16,) * 3
    odd_outs = ((0, nk, 1.0), (nk, nk, 1.0), (2 * nk, nv, 1.0), (2 * nk + nv, nv, 1.0), (2 * nk + 2 * nv, LANES, 1.0))
    odd_dt = (f32,) * 5

    outs = {n: [] for n in ("pk", "pv", "pc", "pn", "pm", "ps", "sk", "sv", "sc", "sn", "sm", "ss")}

    for layer in range(depth):
        gains = norm_gains[layer]
        if layer % 2 == 0:
            e = layer // 2
            w = w_in_even[e]
            w = jnp.concatenate([w[:, :4 * na], w[:, 4 * na + 2 * H_A:], w[:, 4 * na:4 * na + 2 * H_A],
                                 jnp.zeros((D, LANES - 2 * H_A), f32)], axis=1).astype(bf16)
            brow = jnp.pad(b_gate_even[e], (0, LANES - 2 * H_A)).reshape(1, LANES)
            bcol = b_gate_even[e].reshape(2 * H_A, 1)
            gh = g_head_even[e].reshape(1, -1)
            w_out = w_out_even[e].astype(bf16)

            qa, ka, va, oa, gt, kb, vb, qb16, kb16, vb16 = _in_proj(hp, gains[0:1], w, even_outs, even_dt, tm_in)
            r3 = lambda a: a.reshape(B, Lp, a.shape[-1])
            g3 = r3(gt)
            ya, C, n, m = _mlstm(r3(qa), r3(ka), r3(va), r3(oa), g3, jnp.swapaxes(g3[:, :, :2 * H_A], 1, 2),
                                 brow, bcol, jnp.zeros((B, H_A, DK_A, DV_A), f32), jnp.zeros((B, H_A, DK_A), f32),
                                 jnp.zeros((B, H_A, 1), f32), gh, meta_pad, Lp)
            hb = _sb_prompt(r3(qb16), r3(kb16), r3(vb16), b_sb_even[e], meta_pad)
            hp = _post([ya.reshape(B * Lp, -1), hb.reshape(B * Lp, -1)], hp, w_out, gains,
                       w_up[layer].astype(bf16), w_down[layer].astype(bf16), tm_p, POST_FF_TILE)
            outs["pk"].append(kb.reshape(B, Lp, H_B, DH_B)[:, meta_pad:])
            outs["pv"].append(vb.reshape(B, Lp, H_B, DH_B)[:, meta_pad:])
            outs["pc"].append(C)
            outs["pn"].append(n)
            outs["pm"].append(m.reshape(B, H_A))

            qa, ka, va, oa, gt, kb, vb, qb16, kb16, vb16 = _in_proj(hs, gains[0:1], w, even_outs, even_dt, tm_s)
            s3 = lambda a: a.reshape(DB, T, a.shape[-1])
            pc = lambda a: _pad_rows(s3(a), CHUNK_A)
            g3 = pc(gt)
            ya, C, n, m = _mlstm(pc(qa), pc(ka), pc(va), pc(oa), g3, jnp.swapaxes(g3[:, :, :2 * H_A], 1, 2),
                                 brow, bcol, state_mlstm_c[e], state_mlstm_n[e],
                                 state_mlstm_m[e].reshape(DB, H_A, 1), gh, 0, T)
            p8 = lambda a: _pad_rows(s3(a), SUBLANES)
            bias_rows = jnp.broadcast_to(jnp.tile(b_sb_even[e], T)[:, None], (T * H_B, cache_sb_k.shape[2]))
            hb = _sb_sample(p8(qb16.astype(f32)), p8(kb), p8(vb), bias_rows, cache_sb_k, cache_sb_v, page_table, e, T)
            hs = _post([ya[:, :T].reshape(DB * T, -1), hb[:, :T].reshape(DB * T, -1)], hs, w_out, gains,
                       w_up[layer].astype(bf16), w_down[layer].astype(bf16), tm_s, POST_FF_TILE)
            outs["sk"].append(kb.reshape(DB, T, H_B, DH_B))
            outs["sv"].append(vb.reshape(DB, T, H_B, DH_B))
            outs["sc"].append(C)
            outs["sn"].append(n)
            outs["sm"].append(m.reshape(DB, H_A))
        else:
            o = layer // 2
            w = jnp.concatenate([w_in_odd[o], jnp.zeros((D, LANES - GATE_RANK), f32)], axis=1).astype(bf16)
            wg = jnp.concatenate([w_gate_up_odd[o], jnp.zeros((LANES - GATE_RANK, nk), f32)], axis=0).astype(bf16)
            bg = b_gate_up_odd[o].reshape(1, nk)
            gh = g_head_odd[o].reshape(1, -1)
            w_out = w_out_odd[o].astype(bf16)

            q, k, v, r, gd = _in_proj(hp, gains[0:1], w, odd_outs, odd_dt, tm_in)
            r3 = lambda a: a.reshape(B, Lp, a.shape[-1])
            y, S = _gla(r3(q), r3(k), r3(v), r3(r), r3(gd), wg, bg, jnp.zeros((B, H_C, DK_C, DV_C), f32), gh,
                        meta_pad, Lp)
            hp = _post([y.reshape(B * Lp, -1)], hp, w_out, gains,
                       w_up[layer].astype(bf16), w_down[layer].astype(bf16), tm_p, POST_FF_TILE)
            outs["ps"].append(S)

            q, k, v, r, gd = _in_proj(hs, gains[0:1], w, odd_outs, odd_dt, tm_s)
            pc = lambda a: _pad_rows(a.reshape(DB, T, a.shape[-1]), CHUNK_C)
            y, S = _gla(pc(q), pc(k), pc(v), pc(r), pc(gd), wg, bg, state_gla_s[o], gh, 0, T)
            hs = _post([y[:, :T].reshape(DB * T, -1)], hs, w_out, gains,
                       w_up[layer].astype(bf16), w_down[layer].astype(bf16), tm_s, POST_FF_TILE)
            outs["ss"].append(S)

    y_prompt = hp.reshape(B, Lp, D)[:, meta_pad + n_meta:]
    y_sample = hs.reshape(DB, T, D)
    st = jnp.stack
    return (y_prompt, y_sample, st(outs["pk"]), st(outs["pv"]), st(outs["pc"]), st(outs["pn"]), st(outs["pm"]),
            st(outs["ps"]), st(outs["sk"]), st(outs["sv"]), st(outs["sc"]), st(outs["sn"]), st(outs["sm"]),
            st(outs["ss"]))
```

```python
import functools
from typing import Any, NamedTuple

import jax
import jax.numpy as jnp
import numpy as np
from jax import lax
from jax.experimental import pallas as pl
from jax.experimental.pallas import tpu as pltpu

f32 = jnp.float32
bf16 = jnp.bfloat16

BLOCK = 128
H_A, DK_A, DV_A = 4, 128, 128
H_B, DH_B = 8, 64
H_C, DK_C, DV_C = 4, 128, 256
GATE_RANK = 16
GATE_TAU = 16.0
CHUNK_A = 128
CHUNK_C = 64
EPS = 1e-6
LOG2E = 1.4426950408889634

LANES = 128
SUBLANES = 8
VMEM_LIMIT = 48 * 1024 * 1024

_NT = (((1,), (1,)), ((), ()))
_TN = (((0,), (0,)), ((), ()))


def _rms(x, g):
    return x * lax.rsqrt(jnp.mean(x * x, axis=-1, keepdims=True) + EPS) * g


def _softplus(z):
    return jnp.maximum(z, 0.0) + jnp.log(1.0 + jnp.exp2(jnp.abs(z) * (-LOG2E)))


def _log_sigmoid(z):
    return jnp.minimum(z, 0.0) - jnp.log1p(jnp.exp(-jnp.abs(z)))


def _split(x, terms):
    parts = []
    for _ in range(terms - 1):
        p = x.astype(bf16)
        parts.append(p)
        x = x - p.astype(f32)
    parts.append(x.astype(bf16))
    return parts


def _dot01_left(m01, x, terms=3):
    return sum(jnp.dot(m01, p, preferred_element_type=f32) for p in _split(x, terms))


def _dot01_right(x, m01, terms=3):
    return sum(jnp.dot(p, m01, preferred_element_type=f32) for p in _split(x, terms))


def _params(*sem):
    return pltpu.CompilerParams(dimension_semantics=sem, vmem_limit_bytes=VMEM_LIMIT)


class _Out(NamedTuple):
    off: int
    n: int
    dtype: Any = f32
    scale: float = 1.0
    transposed: bool = False


def _in_proj_kernel(x_ref, g_ref, w_ref, *o_refs, outs):
    xn = _rms(x_ref[...], g_ref[...]).astype(bf16)
    cache = {}
    for o_ref, out in zip(o_refs, outs):
        if (out.off, out.n) not in cache:
            cache[(out.off, out.n)] = jnp.dot(xn, w_ref[:, out.off:out.off + out.n], preferred_element_type=f32)
        y = cache[(out.off, out.n)]
        if out.scale != 1.0:
            y = y * out.scale
        if out.transposed:
            o_ref[0] = y.T.astype(o_ref.dtype)
        else:
            o_ref[...] = y.astype(o_ref.dtype)


def _in_proj(x, g, w, outs, tm, seq_len=None):
    m, d = x.shape
    n_tot = w.shape[1]
    seq_len = seq_len or m
    tiles = seq_len // tm
    assert seq_len % tm == 0 and m % seq_len == 0
    return pl.pallas_call(
        functools.partial(_in_proj_kernel, outs=outs),
        grid=(m // tm,),
        in_specs=[pl.BlockSpec((tm, d), lambda i: (i, 0)),
                  pl.BlockSpec((1, d), lambda i: (0, 0)),
                  pl.BlockSpec((d, n_tot), lambda i: (0, 0))],
        out_specs=[pl.BlockSpec((1, o.n, tm), lambda i: (i // tiles, 0, i % tiles)) if o.transposed
                   else pl.BlockSpec((tm, o.n), lambda i: (i, 0)) for o in outs],
        out_shape=[jax.ShapeDtypeStruct((m // seq_len, o.n, seq_len) if o.transposed else (m, o.n), o.dtype)
                   for o in outs],
        compiler_params=_params("parallel"),
        name="in_proj",
    )(x, g, w)


def _post_kernel(*refs, n_a, nf):
    a_refs = refs[:n_a]
    h_ref, wo_ref, g_ref, wu_ref, wd_ref, out_ref, h1_sc, xn_sc, acc_sc = refs[n_a:]
    f = pl.program_id(1)

    @pl.when(f == 0)
    def _():
        if n_a > 1:
            a = jnp.concatenate([r[...] for r in a_refs], axis=-1)
        else:
            a = a_refs[0][...]
        mix = jnp.dot(a, wo_ref[...], preferred_element_type=f32)
        h1 = h_ref[...] + _rms(mix, g_ref[1:2, :])
        h1_sc[...] = h1
        xn_sc[...] = _rms(h1, g_ref[2:3, :]).astype(bf16)
        acc_sc[...] = jnp.zeros_like(acc_sc)

    u = jnp.dot(xn_sc[...], wu_ref[...], preferred_element_type=f32)
    u = jnp.maximum(u, 0.0)
    acc_sc[...] += jnp.dot((u * u).astype(bf16), wd_ref[...], preferred_element_type=f32)

    @pl.when(f == nf - 1)
    def _():
        out_ref[...] = h1_sc[...] + _rms(acc_sc[...], g_ref[3:4, :])


def _post(a_list, h, w_out, gains, w_up, w_down, tm, tf):
    m, d = h.shape
    dff = w_up.shape[1]
    nf = dff // tf
    n_a = len(a_list)
    return pl.pallas_call(
        functools.partial(_post_kernel, n_a=n_a, nf=nf),
        grid=(m // tm, nf),
        in_specs=[pl.BlockSpec((tm, a.shape[1]), lambda i, f: (i, 0)) for a in a_list] + [
            pl.BlockSpec((tm, d), lambda i, f: (i, 0)),
            pl.BlockSpec(w_out.shape, lambda i, f: (0, 0)),
            pl.BlockSpec(gains.shape, lambda i, f: (0, 0)),
            pl.BlockSpec((d, tf), lambda i, f: (0, f)),
            pl.BlockSpec((tf, d), lambda i, f: (f, 0))],
        out_specs=pl.BlockSpec((tm, d), lambda i, f: (i, 0)),
        out_shape=jax.ShapeDtypeStruct((m, d), f32),
        scratch_shapes=[pltpu.VMEM((tm, d), f32), pltpu.VMEM((tm, d), bf16), pltpu.VMEM((tm, d), f32)],
        compiler_params=_params("parallel", "arbitrary"),
        name="post",
    )(*a_list, h, w_out, gains, w_up, w_down)


def _mlstm_kernel(q_ref, k_ref, v_ref, o_ref, g_ref, gt_ref, brow_ref, bcol_ref, c0_ref, n0_ref, m0_ref, gh_ref,
                  y_ref, C_ref, n_ref, m_ref, *, c, lo, hi):
    j = pl.program_id(1)

    @pl.when(j == 0)
    def _():
        C_ref[...] = c0_ref[...]
        n_ref[...] = n0_ref[...]
        m_ref[...] = m0_ref[...]

    row = lax.broadcasted_iota(jnp.int32, (c, c), 0)
    col = lax.broadcasted_iota(jnp.int32, (c, c), 1)
    causal = row >= col
    tril = jnp.where(causal, 1.0, 0.0).astype(bf16)
    triu = jnp.where(row <= col, 1.0, 0.0).astype(bf16)
    pos_c = j * c + lax.broadcasted_iota(jnp.int32, (c, 1), 0)
    valid_c = (pos_c >= lo) & (pos_c < hi)
    pos_r = j * c + lax.broadcasted_iota(jnp.int32, (1, c), 1)
    valid_r = (pos_r >= lo) & (pos_r < hi)

    gc = g_ref[0] + brow_ref[...]
    li_c = jnp.where(valid_c, gc, -jnp.inf)
    lf_c = jnp.where(valid_c, _log_sigmoid(gc), 0.0)
    b_c = _dot01_left(tril, lf_c)
    gr = gt_ref[0] + bcol_ref[...]
    li_r = jnp.where(valid_r, gr, -jnp.inf)
    lf_r = jnp.where(valid_r, _log_sigmoid(gr), 0.0)
    b_r = _dot01_right(lf_r, triu)

    heads = range(H_A)
    sl = [slice(h * DK_A, (h + 1) * DK_A) for h in heads]
    q = [q_ref[0, :, sl[h]] for h in heads]
    k = [k_ref[0, :, sl[h]] * (DK_A ** -0.5) for h in heads]
    qb = [q[h].astype(bf16) for h in heads]
    kb = [k[h].astype(bf16) for h in heads]
    vb = [v_ref[0, :, sl[h]].astype(bf16) for h in heads]
    bc = [b_c[:, H_A + h:H_A + h + 1] for h in heads]
    ic = [li_c[:, h:h + 1] for h in heads]
    br = [b_r[H_A + h:H_A + h + 1, :] for h in heads]
    ir = [li_r[h:h + 1, :] for h in heads]
    m = [m_ref[0, h:h + 1, :] for h in heads]
    C = [C_ref[0, h] for h in heads]
    n = [n_ref[0, h:h + 1, :] for h in heads]

    qk = [lax.dot_general(qb[h], kb[h], _NT, preferred_element_type=f32) for h in heads]
    qc = [jnp.dot(qb[h], C[h].astype(bf16), preferred_element_type=f32) for h in heads]
    d_log = [jnp.where(causal, bc[h] - br[h] + ir[h], -jnp.inf) for h in heads]
    inter = [bc[h] + m[h] for h in heads]
    m_t = [jnp.maximum(inter[h], jnp.max(d_log[h], axis=1, keepdims=True)) for h in heads]
    s = [qk[h] * jnp.exp(d_log[h] - m_t[h]) for h in heads]
    sv = [jnp.dot(s[h].astype(bf16), vb[h], preferred_element_type=f32) for h in heads]
    w_i = [jnp.exp(inter[h] - m_t[h]) for h in heads]
    den = [w_i[h] * jnp.sum(q[h] * n[h], axis=1, keepdims=True) + jnp.sum(s[h], axis=1, keepdims=True)
           for h in heads]

    b_last = [bc[h][c - 1:c, :] for h in heads]
    gcol = [b_last[h] - bc[h] + ic[h] for h in heads]
    m_new = [jnp.maximum(b_last[h] + m[h], jnp.max(gcol[h], axis=0, keepdims=True)) for h in heads]
    w_d = [jnp.exp(b_last[h] + m[h] - m_new[h]) for h in heads]
    kw = [k[h] * jnp.exp(gcol[h] - m_new[h]) for h in heads]
    kv = [lax.dot_general(kw[h].astype(bf16), vb[h], _TN, preferred_element_type=f32) for h in heads]
    for h in heads:
        C_ref[0, h] = w_d[h] * C[h] + kv[h]
        n_ref[0, h:h + 1, :] = w_d[h] * n[h] + jnp.sum(kw[h], axis=0, keepdims=True)
        m_ref[0, h:h + 1, :] = m_new[h]

    for h in heads:
        hh = (w_i[h] * qc[h] + sv[h]) / jnp.maximum(jnp.abs(den[h]), jnp.exp(-m_t[h]))
        y = hh * lax.rsqrt(jnp.mean(hh * hh, axis=1, keepdims=True) + EPS) * gh_ref[:, sl[h]]
        y_ref[0, :, sl[h]] = (y * jax.nn.sigmoid(o_ref[0, :, sl[h]])).astype(y_ref.dtype)


def _mlstm(q, k, v, o, g, gt, brow, bcol, c0, n0, m0, gh, lo, hi):
    nb, length, _ = q.shape
    c = CHUNK_A
    wide = pl.BlockSpec((1, c, H_A * DK_A), lambda b, j: (b, j, 0))
    st = lambda shape: pl.BlockSpec((1,) + shape, lambda b, j: (b,) + (0,) * len(shape))
    const = lambda a: pl.BlockSpec(a.shape, lambda b, j: (0,) * a.ndim)
    return pl.pallas_call(
        functools.partial(_mlstm_kernel, c=c, lo=lo, hi=hi),
        grid=(nb, length // c),
        in_specs=[wide, wide, wide, wide,
                  pl.BlockSpec((1, c, LANES), lambda b, j: (b, j, 0)),
                  pl.BlockSpec((1, 2 * H_A, c), lambda b, j: (b, 0, j)),
                  const(brow), const(bcol),
                  st((H_A, DK_A, DV_A)), st((H_A, DK_A)), st((H_A, 1)), const(gh)],
        out_specs=[wide, st((H_A, DK_A, DV_A)), st((H_A, DK_A)), st((H_A, 1))],
        out_shape=[jax.ShapeDtypeStruct((nb, length, H_A * DV_A), bf16),
                   jax.ShapeDtypeStruct((nb, H_A, DK_A, DV_A), f32),
                   jax.ShapeDtypeStruct((nb, H_A, DK_A), f32),
                   jax.ShapeDtypeStruct((nb, H_A, 1), f32)],
        compiler_params=_params("parallel", "arbitrary"),
        name="mlstm",
    )(q, k, v, o, g, gt, brow, bcol, c0, n0, m0, gh)


SB_HEADS = 4


def _neg_cumsum_table(n):
    j = np.arange(n)[:, None]
    s = np.arange(n)[None, :]
    half = np.concatenate([(j > s).astype(np.float32), np.ones((n, n), np.float32)], axis=1)
    return jnp.asarray(-np.concatenate([half, half], axis=0), bf16)


def _sb_prompt_kernel(bias_ref, q_ref, k_ref, v_ref, tab_ref, o_ref, carry_ref, acc_ref, *, tq, lo):
    hg = pl.program_id(1)
    i = pl.program_id(2)
    sub = tab_ref.shape[0] // 2
    lw = SB_HEADS * DH_B
    lane = lax.broadcasted_iota(jnp.int32, (tq, lw), 1)
    qg = q_ref[0]
    qm = [jnp.where(lane // DH_B == e, qg, jnp.zeros_like(qg)) for e in range(SB_HEADS)]
    bias = [bias_ref[SB_HEADS * hg + e] for e in range(SB_HEADS)]
    heads = range(SB_HEADS)
    carry_ref[...] = jnp.zeros_like(carry_ref)
    acc_ref[...] = jnp.zeros_like(acc_ref)

    def tile(tau, key_bias):
        s0 = pl.multiple_of(tau * tq, tq)
        kt = k_ref[0, pl.ds(s0, tq), :]
        vt = v_ref[0, pl.ds(s0, tq), :]
        z = [lax.dot_general(qm[e], kt, _NT, preferred_element_type=f32) + key_bias[e] for e in heads]
        sp = [_softplus(z[e]) for e in heads]
        hi = [sp[e].astype(bf16) for e in heads]
        lo_ = [(sp[e] - hi[e].astype(f32)).astype(bf16) for e in heads]
        r = [[jnp.dot(jnp.concatenate([hi[e][:, j * sub:(j + 1) * sub], lo_[e][:, j * sub:(j + 1) * sub]], axis=1),
                      tab_ref[...], preferred_element_type=f32) for j in range(tq // sub)] for e in heads]
        t = [z[e] - sp[e] for e in heads]
        for e in heads:
            c = carry_ref[e]
            after = [None] * (tq // sub)
            for j in reversed(range(tq // sub)):
                after[j] = r[e][j][:, :sub] + c
                c = c + r[e][j][:, sub:]
            carry_ref[e] = c
            t[e] = t[e] + jnp.concatenate(after, axis=1)
        w = [jnp.exp(t[e]).astype(bf16) for e in heads]
        for e in heads:
            acc_ref[e] += jnp.dot(w[e], vt, preferred_element_type=f32)

    row = lax.broadcasted_iota(jnp.int32, (tq, tq), 0)
    col = lax.broadcasted_iota(jnp.int32, (tq, tq), 1)
    readable = (col < row) & (i * tq + col >= lo)
    tile(i, [jnp.where(readable, b, -jnp.inf) for b in bias])

    def body(n, carry):
        tau = i - 1 - n
        spos = tau * tq + lax.broadcasted_iota(jnp.int32, (1, tq), 1)
        tile(tau, [jnp.where(spos >= lo, b, -jnp.inf) for b in bias])
        return carry

    lax.fori_loop(0, i, body, 0)
    out = acc_ref[0]
    for e in range(1, SB_HEADS):
        out = jnp.where(lane // DH_B == e, acc_ref[e], out)
    o_ref[0] = out.astype(o_ref.dtype)


def _sb_prompt(q, k, v, bias, lo):
    nb, length, width = q.shape
    nblk = length // BLOCK
    tq = BLOCK * max(d for d in (1, 2, 3, 4) if nblk % d == 0)
    lw = SB_HEADS * DH_B
    blk = pl.BlockSpec((1, tq, lw), lambda b, hg, i: (b, i, hg))
    full = pl.BlockSpec((1, length, lw), lambda b, hg, i: (b, 0, hg))
    tab = _neg_cumsum_table(BLOCK)
    return pl.pallas_call(
        functools.partial(_sb_prompt_kernel, tq=tq, lo=lo),
        grid=(nb, width // lw, length // tq),
        in_specs=[pl.BlockSpec(memory_space=pltpu.SMEM), blk, full, full,
                  pl.BlockSpec(tab.shape, lambda b, hg, i: (0, 0))],
        out_specs=blk,
        out_shape=jax.ShapeDtypeStruct((nb, length, width), bf16),
        scratch_shapes=[pltpu.VMEM((SB_HEADS, tq, BLOCK), f32), pltpu.VMEM((SB_HEADS, tq, lw), f32)],
        compiler_params=_params("parallel", "parallel", "arbitrary"),
        name="sb_prompt",
    )(bias, q, k, v, tab)


QH = 32


def _sb_sample_kernel(pt_ref, q_ref, kn_ref, vn_ref, bias_ref, tab_ref, *rest, n_pg, n_grp, n_tok):
    k_refs = rest[:n_pg]
    v_refs = rest[n_pg:2 * n_pg]
    o_ref, carry_sc, acc_sc = rest[2 * n_pg:]
    g = pl.program_id(1)
    width = H_B * DH_B
    psz = tab_ref.shape[0] // 2
    row_head = lax.broadcasted_iota(jnp.int32, (QH, width), 0) % H_B
    lane_head = lax.broadcasted_iota(jnp.int32, (QH, width), 1) // DH_B
    q = q_ref[0]
    qrows = jnp.concatenate([jnp.broadcast_to(q[t:t + 1, :], (H_B, width)) for t in range(n_tok)], axis=0)
    qcat = jnp.where(lane_head == row_head, qrows, jnp.zeros_like(qrows)).astype(bf16)

    def weights(zs):
        n = len(zs)
        z = jnp.concatenate(zs, axis=0) if n > 1 else zs[0]
        sp = _softplus(z)
        hi = sp.astype(bf16)
        lo_ = (sp - hi.astype(f32)).astype(bf16)
        r = jnp.dot(jnp.concatenate([hi, lo_], axis=1), tab_ref[...], preferred_element_type=f32)
        c = carry_sc[...]
        after = [None] * n
        for p in reversed(range(n)):
            after[p] = r[p * QH:(p + 1) * QH, :psz] + c
            c = c + r[p * QH:(p + 1) * QH, psz:]
        carry_sc[...] = c
        w = jnp.exp(z - sp + (jnp.concatenate(after, axis=0) if n > 1 else after[0])).astype(bf16)
        return [w[p * QH:(p + 1) * QH, :] for p in range(n)]

    @pl.when(g == 0)
    def _():
        carry_sc[...] = jnp.zeros_like(carry_sc)
        pad = jnp.zeros((psz - kn_ref.shape[1], width), f32)
        kn = jnp.concatenate([kn_ref[0], pad], axis=0).astype(bf16)
        vn = jnp.concatenate([vn_ref[0], pad], axis=0).astype(bf16)
        j = lax.broadcasted_iota(jnp.int32, (QH, psz), 1)
        t = lax.broadcasted_iota(jnp.int32, (QH, psz), 0) // H_B
        z = lax.dot_general(qcat, kn, _NT, preferred_element_type=f32) + jnp.where(j < t, bias_ref[...], -jnp.inf)
        acc_sc[...] = jnp.dot(weights([z])[0], vn, preferred_element_type=f32)

    ws = weights([jnp.dot(qcat, k_refs[p][0, 0].astype(bf16), preferred_element_type=f32) + bias_ref[...]
                  for p in range(n_pg)])
    acc = acc_sc[...]
    for p in range(n_pg):
        acc = acc + lax.dot_general(ws[p], v_refs[p][0, 0].astype(bf16), _NT, preferred_element_type=f32)
    acc_sc[...] = acc

    @pl.when(g == n_grp - 1)
    def _():
        own = jnp.where(lane_head == row_head, acc_sc[...], 0.0)
        rows = [jnp.sum(own[t * H_B:(t + 1) * H_B, :], axis=0, keepdims=True) for t in range(n_tok)]
        rows.append(jnp.zeros((o_ref.shape[1] - n_tok, width), f32))
        o_ref[0] = jnp.concatenate(rows, axis=0).astype(o_ref.dtype)


def _sb_sample(q, kn, vn, bias, cache_k, cache_v, page_table, layer, n_tok, n_pg=16):
    db, rows, width = q.shape
    n_pages = page_table.shape[1]
    n_layers, n_pool, psz = cache_k.shape[:3]
    n_grp = n_pages // n_pg
    assert n_tok * H_B == QH and n_pages % n_pg == 0
    tab = _neg_cumsum_table(psz)
    cache_k = jnp.transpose(cache_k, (0, 1, 3, 4, 2)).reshape(n_layers, n_pool, width, psz)
    cache_v = jnp.transpose(cache_v, (0, 1, 3, 4, 2)).reshape(n_layers, n_pool, width, psz)

    def page_spec(p):
        return pl.BlockSpec((1, 1, width, psz),
                            lambda b, g, pt: (layer, pt[b, (n_grp - 1 - g) * n_pg + p], 0, 0))

    small = pl.BlockSpec((1, rows, width), lambda b, g, pt: (b, 0, 0))
    grid_spec = pltpu.PrefetchScalarGridSpec(
        num_scalar_prefetch=1,
        grid=(db, n_grp),
        in_specs=[small, small, small,
                  pl.BlockSpec(bias.shape, lambda b, g, pt: (0, 0)),
                  pl.BlockSpec(tab.shape, lambda b, g, pt: (0, 0))]
        + [page_spec(p) for p in range(n_pg)] * 2,
        out_specs=small,
        scratch_shapes=[pltpu.VMEM((QH, psz), f32), pltpu.VMEM((QH, width), f32)])
    return pl.pallas_call(
        functools.partial(_sb_sample_kernel, n_pg=n_pg, n_grp=n_grp, n_tok=n_tok),
        grid_spec=grid_spec,
        out_shape=jax.ShapeDtypeStruct((db, rows, width), bf16),
        compiler_params=_params("parallel", "arbitrary"),
        name="sb_sample",
    )(page_table, q, kn, vn, bias, tab, *([cache_k] * n_pg), *([cache_v] * n_pg))


GLA_CHUNKS_PER_STEP = 2


def _gla_tables(c):
    t = np.arange(c)[:, None]
    j = np.arange(c)[None, :]
    mats = [(j <= t), (j > t)]
    s = 1
    while s < c:
        bound = (t // (2 * s)) * (2 * s) + s - 1
        right = (t // s) % 2 == 1
        mats.append(np.where(right, (j > bound) & (j <= t), (j > t) & (j <= bound)))
        s *= 2
    return jnp.asarray(np.concatenate(mats, axis=0).astype(np.float32), bf16)


def _gla_kernel(q_ref, k_ref, v_ref, r_ref, gd_ref, wg_ref, bg_ref, e_ref, s0_ref, gh_ref,
                y_ref, S_ref, *, c, nck, lo, hi):
    j = pl.program_id(1)

    @pl.when(j == 0)
    def _():
        S_ref[...] = s0_ref[...]


    rows = nck * c
    pos = j * rows + lax.broadcasted_iota(jnp.int32, (rows, 1), 0)
    valid = (pos >= lo) & (pos < hi)
    rowc = lax.broadcasted_iota(jnp.int32, (c, 1), 0)
    row = lax.broadcasted_iota(jnp.int32, (c, c), 0)
    col = lax.broadcasted_iota(jnp.int32, (c, c), 1)
    lg = _log_sigmoid(jnp.dot(gd_ref[0].astype(bf16), wg_ref[...], preferred_element_type=f32)
                      + bg_ref[...]) / GATE_TAU
    lg = jnp.where(valid, lg, 0.0)
    n_lvl = e_ref.shape[0] // c - 2

    heads = range(H_C)
    items = [(h, u) for h in heads for u in range(nck)]
    sk = {(h, u): (slice(u * c, (u + 1) * c), slice(h * DK_C, (h + 1) * DK_C)) for h, u in items}
    sv = {(h, u): (slice(u * c, (u + 1) * c), slice(h * DV_C, (h + 1) * DV_C)) for h, u in items}
    q = {i: q_ref[0, sk[i][0], sk[i][1]] * (DK_C ** -0.5) for i in items}
    k = {i: jnp.where(valid[sk[i][0], :], k_ref[0, sk[i][0], sk[i][1]], 0.0) for i in items}
    vb = {i: v_ref[0, sv[i][0], sv[i][1]].astype(bf16) for i in items}
    x = {i: _dot01_left(e_ref[...], lg[sk[i][0], sk[i][1]], terms=2) for i in items}
    fx = {i: jnp.exp(x[i]) for i in items}
    kv = {i: lax.dot_general((k[i] * fx[i][c:2 * c, :]).astype(bf16), vb[i], _TN, preferred_element_type=f32)
          for i in items}
    a = {i: jnp.where(row == col, jnp.sum(q[i] * k[i], axis=1, keepdims=True), 0.0) for i in items}
    for lvl in range(n_lvl):
        s = 1 << lvl
        right = (rowc // s) % 2 == 1
        same = row // (2 * s) == col // (2 * s)
        fl = {i: fx[i][(2 + lvl) * c:(3 + lvl) * c, :] for i in items}
        qt = {i: jnp.where(right, q[i] * fl[i], 0.0).astype(bf16) for i in items}
        kt = {i: jnp.where(right, 0.0, k[i] * fl[i]).astype(bf16) for i in items}
        p = {i: lax.dot_general(qt[i], kt[i], _NT, preferred_element_type=f32) for i in items}
        a = {i: a[i] + jnp.where(same, p[i], 0.0) for i in items}
    o = {i: jnp.dot(a[i].astype(bf16), vb[i], preferred_element_type=f32) for i in items}
    for h in heads:
        S = S_ref[0, h]
        for u in range(nck):
            i = (h, u)
            o[i] = o[i] + jnp.dot((q[i] * fx[i][0:c, :]).astype(bf16), S.astype(bf16), preferred_element_type=f32)
            b_last = x[i][c - 1:c, :]
            decay = jnp.exp(jnp.transpose(jnp.broadcast_to(b_last, (DK_C, DK_C))))
            S = jnp.concatenate([decay] * (DV_C // DK_C), axis=1) * S + kv[i]
        S_ref[0, h] = S
    for i in items:
        y = o[i] * lax.rsqrt(jnp.mean(o[i] * o[i], axis=1, keepdims=True) + EPS) * gh_ref[:, sv[i][1]]
        y_ref[0, sv[i][0], sv[i][1]] = (y * jax.nn.silu(r_ref[0, sv[i][0], sv[i][1]])).astype(y_ref.dtype)


def _gla(q, k, v, r, gd, wg, bg, s0, gh, lo, hi):
    nb, length, _ = q.shape
    c = CHUNK_C
    nck = GLA_CHUNKS_PER_STEP
    e = _gla_tables(c)
    blk = lambda w: pl.BlockSpec((1, nck * c, w), lambda b, j: (b, j, 0))
    const = lambda a: pl.BlockSpec(a.shape, lambda b, j: (0,) * a.ndim)
    st = pl.BlockSpec((1, H_C, DK_C, DV_C), lambda b, j: (b, 0, 0, 0))
    return pl.pallas_call(
        functools.partial(_gla_kernel, c=c, nck=nck, lo=lo, hi=hi),
        grid=(nb, length // (nck * c)),
        in_specs=[blk(H_C * DK_C), blk(H_C * DK_C), blk(H_C * DV_C), blk(H_C * DV_C), blk(LANES),
                  const(wg), const(bg), const(e), st, const(gh)],
        out_specs=[blk(H_C * DV_C), st],
        out_shape=[jax.ShapeDtypeStruct((nb, length, H_C * DV_C), bf16),
                   jax.ShapeDtypeStruct((nb, H_C, DK_C, DV_C), f32)],
        compiler_params=_params("parallel", "arbitrary"),
        name="gla",
    )(q, k, v, r, gd, wg, bg, e, s0, gh)


def _pad_rows(x, rows):
    return jnp.pad(x, ((0, 0), (0, rows - x.shape[1]), (0, 0)))


POST_ROW_TILES = (768, 512, 256, 128)
POST_FF_TILE = 1024


def _row_tile(m, candidates):
    for tm in candidates:
        if m % tm == 0:
            return tm
    raise ValueError(f"row count {m} is not a multiple of {candidates[-1]}")


def kernel(x_prompt, x_sample, cache_sb_k, cache_sb_v, state_mlstm_c, state_mlstm_n, state_mlstm_m, state_gla_s,
           page_table, meta_tokens, norm_gains, w_in_even, b_gate_even, b_sb_even, g_head_even, w_out_even,
           w_in_odd, w_gate_up_odd, b_gate_up_odd, g_head_odd, w_out_odd, w_up, w_down):
    B, seq, D = x_prompt.shape
    DB, T, _ = x_sample.shape
    depth = norm_gains.shape[0]
    n_meta = meta_tokens.shape[0]
    meta_pad = BLOCK - n_meta
    Lp = meta_pad + n_meta + seq
    assert Lp % BLOCK == 0 and (DB * T) % LANES == 0 and T <= SUBLANES

    hp = jnp.concatenate([jnp.zeros((B, meta_pad, D), f32),
                          jnp.broadcast_to(meta_tokens[None], (B, n_meta, D)), x_prompt], axis=1)
    hp = hp.reshape(B * Lp, D)
    hs = x_sample.reshape(DB * T, D)
    tm_p, tm_s = _row_tile(B * Lp, POST_ROW_TILES), _row_tile(DB * T, POST_ROW_TILES)
    tm_in = _row_tile(Lp, (384, 256, 128))

    na, nb_ = H_A * DK_A, H_B * DH_B
    nk, nv = H_C * DK_C, H_C * DV_C
    o_qa, o_ka, o_va, o_oa = 0, na, 2 * na, 3 * na
    o_qb, o_kb, o_vb, o_g = 4 * na, 4 * na + nb_, 4 * na + 2 * nb_, 4 * na + 3 * nb_
    even_common = (_Out(o_qa, na), _Out(o_ka, na), _Out(o_va, na), _Out(o_oa, na), _Out(o_g, LANES),
                   _Out(o_qb, nb_, bf16, DH_B ** -0.5), _Out(o_kb, nb_, bf16), _Out(o_vb, nb_, bf16))
    even_outs_p = even_common + (_Out(o_kb, nb_, transposed=True), _Out(o_vb, nb_, transposed=True))
    even_outs_s = even_common + (_Out(o_kb, nb_), _Out(o_vb, nb_))
    odd_outs = (_Out(0, nk), _Out(nk, nk), _Out(2 * nk, nv), _Out(2 * nk + nv, nv), _Out(2 * nk + 2 * nv, LANES))

    outs = {n: [] for n in ("pk", "pv", "pc", "pn", "pm", "ps", "sk", "sv", "sc", "sn", "sm", "ss")}

    for layer in range(depth):
        gains = norm_gains[layer]
        if layer % 2 == 0:
            e = layer // 2
            w = w_in_even[e]
            w = jnp.concatenate([w[:, :4 * na], w[:, 4 * na + 2 * H_A:], w[:, 4 * na:4 * na + 2 * H_A],
                                 jnp.zeros((D, LANES - 2 * H_A), f32)], axis=1).astype(bf16)
            brow = jnp.pad(b_gate_even[e], (0, LANES - 2 * H_A)).reshape(1, LANES)
            bcol = b_gate_even[e].reshape(2 * H_A, 1)
            gh = g_head_even[e].reshape(1, -1)
            w_out = w_out_even[e].astype(bf16)

            qa, ka, va, oa, gt, qb16, kb16, vb16, kbt, vbt = _in_proj(hp, gains[0:1], w, even_outs_p, tm_in, Lp)
            r3 = lambda a: a.reshape(B, Lp, a.shape[-1])
            g3 = r3(gt)
            ya, C, n, m = _mlstm(r3(qa), r3(ka), r3(va), r3(oa), g3, jnp.swapaxes(g3[:, :, :2 * H_A], 1, 2),
                                 brow, bcol, jnp.zeros((B, H_A, DK_A, DV_A), f32), jnp.zeros((B, H_A, DK_A), f32),
                                 jnp.zeros((B, H_A, 1), f32), gh, meta_pad, Lp)
            hb = _sb_prompt(r3(qb16), r3(kb16), r3(vb16), b_sb_even[e], meta_pad)
            hp = _post([ya.reshape(B * Lp, -1), hb.reshape(B * Lp, -1)], hp, w_out, gains,
                       w_up[layer].astype(bf16), w_down[layer].astype(bf16), tm_p, POST_FF_TILE)
            rows = lambda t: jnp.transpose(t.reshape(B, H_B, DH_B, Lp)[..., meta_pad:], (0, 3, 1, 2))
            outs["pk"].append(rows(kbt))
            outs["pv"].append(rows(vbt))
            outs["pc"].append(C)
            outs["pn"].append(n)
            outs["pm"].append(m.reshape(B, H_A))

            qa, ka, va, oa, gt, qb16, kb16, vb16, kb, vb = _in_proj(hs, gains[0:1], w, even_outs_s, tm_s)
            s3 = lambda a: a.reshape(DB, T, a.shape[-1])
            pc = lambda a: _pad_rows(s3(a), CHUNK_A)
            g3 = pc(gt)
            ya, C, n, m = _mlstm(pc(qa), pc(ka), pc(va), pc(oa), g3, jnp.swapaxes(g3[:, :, :2 * H_A], 1, 2),
                                 brow, bcol, state_mlstm_c[e], state_mlstm_n[e],
                                 state_mlstm_m[e].reshape(DB, H_A, 1), gh, 0, T)
            p8 = lambda a: _pad_rows(s3(a), SUBLANES)
            bias_rows = jnp.broadcast_to(jnp.tile(b_sb_even[e], T)[:, None], (T * H_B, cache_sb_k.shape[2]))
            hb = _sb_sample(p8(qb16.astype(f32)), p8(kb), p8(vb), bias_rows, cache_sb_k, cache_sb_v, page_table, e, T)
            hs = _post([ya[:, :T].reshape(DB * T, -1), hb[:, :T].reshape(DB * T, -1)], hs, w_out, gains,
                       w_up[layer].astype(bf16), w_down[layer].astype(bf16), tm_s, POST_FF_TILE)
            outs["sk"].append(kb.reshape(DB, T, H_B, DH_B))
            outs["sv"].append(vb.reshape(DB, T, H_B, DH_B))
            outs["sc"].append(C)
            outs["sn"].append(n)
            outs["sm"].append(m.reshape(DB, H_A))
        else:
            o = layer // 2
            w = jnp.concatenate([w_in_odd[o], jnp.zeros((D, LANES - GATE_RANK), f32)], axis=1).astype(bf16)
            wg = jnp.concatenate([w_gate_up_odd[o], jnp.zeros((LANES - GATE_RANK, nk), f32)], axis=0).astype(bf16)
            bg = b_gate_up_odd[o].reshape(1, nk)
            gh = g_head_odd[o].reshape(1, -1)
            w_out = w_out_odd[o].astype(bf16)

            q, k, v, r, gd = _in_proj(hp, gains[0:1], w, odd_outs, tm_in)
            r3 = lambda a: a.reshape(B, Lp, a.shape[-1])
            y, S = _gla(r3(q), r3(k), r3(v), r3(r), r3(gd), wg, bg, jnp.zeros((B, H_C, DK_C, DV_C), f32), gh,
                        meta_pad, Lp)
            hp = _post([y.reshape(B * Lp, -1)], hp, w_out, gains,
                       w_up[layer].astype(bf16), w_down[layer].astype(bf16), tm_p, POST_FF_TILE)
            outs["ps"].append(S)

            q, k, v, r, gd = _in_proj(hs, gains[0:1], w, odd_outs, tm_s)
            pc = lambda a: _pad_rows(a.reshape(DB, T, a.shape[-1]), GLA_CHUNKS_PER_STEP * CHUNK_C)
            y, S = _gla(pc(q), pc(k), pc(v), pc(r), pc(gd), wg, bg, state_gla_s[o], gh, 0, T)
            hs = _post([y[:, :T].reshape(DB * T, -1)], hs, w_out, gains,
                       w_up[layer].astype(bf16), w_down[layer].astype(bf16), tm_s, POST_FF_TILE)
            outs["ss"].append(S)

    y_prompt = hp.reshape(B, Lp, D)[:, meta_pad + n_meta:]
    y_sample = hs.reshape(DB, T, D)
    st = jnp.stack
    return (y_prompt, y_sample, st(outs["pk"]), st(outs["pv"]), st(outs["pc"]), st(outs["pn"]), st(outs["pm"]),
            st(outs["ps"]), st(outs["sk"]), st(outs["sv"]), st(outs["sc"]), st(outs["sn"]), st(outs["sm"]),
            st(outs["ss"]))
```

```python
import functools
from typing import Any, NamedTuple

import jax
import jax.numpy as jnp
import numpy as np
from jax import lax
from jax.experimental import pallas as pl
from jax.experimental.pallas import tpu as pltpu

f32 = jnp.float32
bf16 = jnp.bfloat16

BLOCK = 128
H_A, DK_A, DV_A = 4, 128, 128
H_B, DH_B = 8, 64
H_C, DK_C, DV_C = 4, 128, 256
GATE_RANK = 16
GATE_TAU = 16.0
CHUNK_A = 128
CHUNK_C = 64
EPS = 1e-6
LOG2E = 1.4426950408889634

LANES = 128
SUBLANES = 8
VMEM_LIMIT = 48 * 1024 * 1024

_NT = (((1,), (1,)), ((), ()))
_TN = (((0,), (0,)), ((), ()))


def _rms(x, g):
    return x * lax.rsqrt(jnp.mean(x * x, axis=-1, keepdims=True) + EPS) * g


def _softplus(z):
    return jnp.maximum(z, 0.0) + jnp.log(1.0 + jnp.exp2(jnp.abs(z) * (-LOG2E)))


def _log_sigmoid(z):
    return jnp.minimum(z, 0.0) - jnp.log1p(jnp.exp(-jnp.abs(z)))


def _split(x, terms):
    parts = []
    for _ in range(terms - 1):
        p = x.astype(bf16)
        parts.append(p)
        x = x - p.astype(f32)
    parts.append(x.astype(bf16))
    return parts


def _dot01_left(m01, x, terms=3):
    return sum(jnp.dot(m01, p, preferred_element_type=f32) for p in _split(x, terms))


def _dot01_right(x, m01, terms=3):
    return sum(jnp.dot(p, m01, preferred_element_type=f32) for p in _split(x, terms))


def _params(*sem):
    return pltpu.CompilerParams(dimension_semantics=sem, vmem_limit_bytes=VMEM_LIMIT)


class _Out(NamedTuple):
    off: int
    n: int
    dtype: Any = f32
    scale: float = 1.0
    transposed: bool = False


def _in_proj_kernel(x_ref, g_ref, w_ref, *o_refs, outs):
    xn = _rms(x_ref[...], g_ref[...]).astype(bf16)
    cache = {}
    for o_ref, out in zip(o_refs, outs):
        if (out.off, out.n) not in cache:
            cache[(out.off, out.n)] = jnp.dot(xn, w_ref[:, out.off:out.off + out.n], preferred_element_type=f32)
        y = cache[(out.off, out.n)]
        if out.scale != 1.0:
            y = y * out.scale
        if out.transposed:
            o_ref[0] = y.T.astype(o_ref.dtype)
        else:
            o_ref[...] = y.astype(o_ref.dtype)


def _in_proj(x, g, w, outs, tm, seq_len=None):
    m, d = x.shape
    n_tot = w.shape[1]
    seq_len = seq_len or m
    tiles = seq_len // tm
    assert seq_len % tm == 0 and m % seq_len == 0
    return pl.pallas_call(
        functools.partial(_in_proj_kernel, outs=outs),
        grid=(m // tm,),
        in_specs=[pl.BlockSpec((tm, d), lambda i: (i, 0)),
                  pl.BlockSpec((1, d), lambda i: (0, 0)),
                  pl.BlockSpec((d, n_tot), lambda i: (0, 0))],
        out_specs=[pl.BlockSpec((1, o.n, tm), lambda i: (i // tiles, 0, i % tiles)) if o.transposed
                   else pl.BlockSpec((tm, o.n), lambda i: (i, 0)) for o in outs],
        out_shape=[jax.ShapeDtypeStruct((m // seq_len, o.n, seq_len) if o.transposed else (m, o.n), o.dtype)
                   for o in outs],
        compiler_params=_params("parallel"),
        name="in_proj",
    )(x, g, w)


def _post_kernel(*refs, n_a, nf):
    a_refs = refs[:n_a]
    h_ref, wo_ref, g_ref, wu_ref, wd_ref, out_ref, h1_sc, xn_sc, acc_sc = refs[n_a:]
    f = pl.program_id(1)

    @pl.when(f == 0)
    def _():
        if n_a > 1:
            a = jnp.concatenate([r[...] for r in a_refs], axis=-1)
        else:
            a = a_refs[0][...]
        mix = jnp.dot(a, wo_ref[...], preferred_element_type=f32)
        h1 = h_ref[...] + _rms(mix, g_ref[1:2, :])
        h1_sc[...] = h1
        xn_sc[...] = _rms(h1, g_ref[2:3, :]).astype(bf16)
        acc_sc[...] = jnp.zeros_like(acc_sc)

    u = jnp.dot(xn_sc[...], wu_ref[...], preferred_element_type=f32)
    u = jnp.maximum(u, 0.0)
    acc_sc[...] += jnp.dot((u * u).astype(bf16), wd_ref[...], preferred_element_type=f32)

    @pl.when(f == nf - 1)
    def _():
        out_ref[...] = h1_sc[...] + _rms(acc_sc[...], g_ref[3:4, :])


def _post(a_list, h, w_out, gains, w_up, w_down, tm, tf):
    m, d = h.shape
    dff = w_up.shape[1]
    nf = dff // tf
    n_a = len(a_list)
    return pl.pallas_call(
        functools.partial(_post_kernel, n_a=n_a, nf=nf),
        grid=(m // tm, nf),
        in_specs=[pl.BlockSpec((tm, a.shape[1]), lambda i, f: (i, 0)) for a in a_list] + [
            pl.BlockSpec((tm, d), lambda i, f: (i, 0)),
            pl.BlockSpec(w_out.shape, lambda i, f: (0, 0)),
            pl.BlockSpec(gains.shape, lambda i, f: (0, 0)),
            pl.BlockSpec((d, tf), lambda i, f: (0, f)),
            pl.BlockSpec((tf, d), lambda i, f: (f, 0))],
        out_specs=pl.BlockSpec((tm, d), lambda i, f: (i, 0)),
        out_shape=jax.ShapeDtypeStruct((m, d), f32),
        scratch_shapes=[pltpu.VMEM((tm, d), f32), pltpu.VMEM((tm, d), bf16), pltpu.VMEM((tm, d), f32)],
        compiler_params=_params("parallel", "arbitrary"),
        name="post",
    )(*a_list, h, w_out, gains, w_up, w_down)


def _mlstm_kernel(q_ref, k_ref, v_ref, o_ref, g_ref, gt_ref, brow_ref, bcol_ref, c0_ref, n0_ref, m0_ref, gh_ref,
                  y_ref, C_ref, n_ref, m_ref, *, c, lo, hi):
    j = pl.program_id(1)

    @pl.when(j == 0)
    def _():
        C_ref[...] = c0_ref[...]
        n_ref[...] = n0_ref[...]
        m_ref[...] = m0_ref[...]

    row = lax.broadcasted_iota(jnp.int32, (c, c), 0)
    col = lax.broadcasted_iota(jnp.int32, (c, c), 1)
    causal = row >= col
    tril = jnp.where(causal, 1.0, 0.0).astype(bf16)
    triu = jnp.where(row <= col, 1.0, 0.0).astype(bf16)
    pos_c = j * c + lax.broadcasted_iota(jnp.int32, (c, 1), 0)
    valid_c = (pos_c >= lo) & (pos_c < hi)
    pos_r = j * c + lax.broadcasted_iota(jnp.int32, (1, c), 1)
    valid_r = (pos_r >= lo) & (pos_r < hi)

    gc = g_ref[0] + brow_ref[...]
    li_c = jnp.where(valid_c, gc, -jnp.inf)
    lf_c = jnp.where(valid_c, _log_sigmoid(gc), 0.0)
    b_c = _dot01_left(tril, lf_c)
    gr = gt_ref[0] + bcol_ref[...]
    li_r = jnp.where(valid_r, gr, -jnp.inf)
    lf_r = jnp.where(valid_r, _log_sigmoid(gr), 0.0)
    b_r = _dot01_right(lf_r, triu)

    heads = range(H_A)
    sl = [slice(h * DK_A, (h + 1) * DK_A) for h in heads]
    q = [q_ref[0, :, sl[h]] for h in heads]
    k = [k_ref[0, :, sl[h]] * (DK_A ** -0.5) for h in heads]
    qb = [q[h].astype(bf16) for h in heads]
    kb = [k[h].astype(bf16) for h in heads]
    vb = [v_ref[0, :, sl[h]].astype(bf16) for h in heads]
    bc = [b_c[:, H_A + h:H_A + h + 1] for h in heads]
    ic = [li_c[:, h:h + 1] for h in heads]
    br = [b_r[H_A + h:H_A + h + 1, :] for h in heads]
    ir = [li_r[h:h + 1, :] for h in heads]
    m = [m_ref[0, h:h + 1, :] for h in heads]
    C = [C_ref[0, h] for h in heads]
    n = [n_ref[0, h:h + 1, :] for h in heads]

    qk = [lax.dot_general(qb[h], kb[h], _NT, preferred_element_type=f32) for h in heads]
    qc = [jnp.dot(qb[h], C[h].astype(bf16), preferred_element_type=f32) for h in heads]
    d_log = [jnp.where(causal, bc[h] - br[h] + ir[h], -jnp.inf) for h in heads]
    inter = [bc[h] + m[h] for h in heads]
    m_t = [jnp.maximum(inter[h], jnp.max(d_log[h], axis=1, keepdims=True)) for h in heads]
    s = [qk[h] * jnp.exp(d_log[h] - m_t[h]) for h in heads]
    sv = [jnp.dot(s[h].astype(bf16), vb[h], preferred_element_type=f32) for h in heads]
    w_i = [jnp.exp(inter[h] - m_t[h]) for h in heads]
    den = [w_i[h] * jnp.sum(q[h] * n[h], axis=1, keepdims=True) + jnp.sum(s[h], axis=1, keepdims=True)
           for h in heads]

    b_last = [bc[h][c - 1:c, :] for h in heads]
    gcol = [b_last[h] - bc[h] + ic[h] for h in heads]
    m_new = [jnp.maximum(b_last[h] + m[h], jnp.max(gcol[h], axis=0, keepdims=True)) for h in heads]
    w_d = [jnp.exp(b_last[h] + m[h] - m_new[h]) for h in heads]
    kw = [k[h] * jnp.exp(gcol[h] - m_new[h]) for h in heads]
    kv = [lax.dot_general(kw[h].astype(bf16), vb[h], _TN, preferred_element_type=f32) for h in heads]
    for h in heads:
        C_ref[0, h] = w_d[h] * C[h] + kv[h]
        n_ref[0, h:h + 1, :] = w_d[h] * n[h] + jnp.sum(kw[h], axis=0, keepdims=True)
        m_ref[0, h:h + 1, :] = m_new[h]

    for h in heads:
        hh = (w_i[h] * qc[h] + sv[h]) / jnp.maximum(jnp.abs(den[h]), jnp.exp(-m_t[h]))
        y = hh * lax.rsqrt(jnp.mean(hh * hh, axis=1, keepdims=True) + EPS) * gh_ref[:, sl[h]]
        y_ref[0, :, sl[h]] = (y * jax.nn.sigmoid(o_ref[0, :, sl[h]])).astype(y_ref.dtype)


def _mlstm(q, k, v, o, g, gt, brow, bcol, c0, n0, m0, gh, lo, hi):
    nb, length, _ = q.shape
    c = CHUNK_A
    wide = pl.BlockSpec((1, c, H_A * DK_A), lambda b, j: (b, j, 0))
    st = lambda shape: pl.BlockSpec((1,) + shape, lambda b, j: (b,) + (0,) * len(shape))
    const = lambda a: pl.BlockSpec(a.shape, lambda b, j: (0,) * a.ndim)
    return pl.pallas_call(
        functools.partial(_mlstm_kernel, c=c, lo=lo, hi=hi),
        grid=(nb, length // c),
        in_specs=[wide, wide, wide, wide,
                  pl.BlockSpec((1, c, LANES), lambda b, j: (b, j, 0)),
                  pl.BlockSpec((1, 2 * H_A, c), lambda b, j: (b, 0, j)),
                  const(brow), const(bcol),
                  st((H_A, DK_A, DV_A)), st((H_A, DK_A)), st((H_A, 1)), const(gh)],
        out_specs=[wide, st((H_A, DK_A, DV_A)), st((H_A, DK_A)), st((H_A, 1))],
        out_shape=[jax.ShapeDtypeStruct((nb, length, H_A * DV_A), bf16),
                   jax.ShapeDtypeStruct((nb, H_A, DK_A, DV_A), f32),
                   jax.ShapeDtypeStruct((nb, H_A, DK_A), f32),
                   jax.ShapeDtypeStruct((nb, H_A, 1), f32)],
        compiler_params=_params("parallel", "arbitrary"),
        name="mlstm",
    )(q, k, v, o, g, gt, brow, bcol, c0, n0, m0, gh)


SB_HEADS = 4


def _neg_cumsum_table(n, inclusive=False):
    j = np.arange(n)[:, None]
    s = np.arange(n)[None, :]
    later = (j >= s) if inclusive else (j > s)
    half = np.concatenate([later.astype(np.float32), np.ones((n, n), np.float32)], axis=1)
    return jnp.asarray(-np.concatenate([half, half], axis=0), bf16)


def _sb_prompt_kernel(bias_ref, q_ref, k_ref, v_ref, tab_ref, o_ref, carry_ref, acc_ref, *, tq, lo):
    hg = pl.program_id(1)
    i = pl.program_id(2)
    sub = tab_ref.shape[0] // 2
    lw = SB_HEADS * DH_B
    lane = lax.broadcasted_iota(jnp.int32, (tq, lw), 1)
    qg = q_ref[0]
    qm = [jnp.where(lane // DH_B == e, qg, jnp.zeros_like(qg)) for e in range(SB_HEADS)]
    bias = [bias_ref[SB_HEADS * hg + e] for e in range(SB_HEADS)]
    heads = range(SB_HEADS)
    carry_ref[...] = jnp.zeros_like(carry_ref)
    acc_ref[...] = jnp.zeros_like(acc_ref)

    def keys(ref, tau):
        return ref[0, pl.ds(pl.multiple_of(tau * tq, tq), tq), :]

    nsub = tq // sub

    def sweep(taus, key_bias):
        nt = len(taus)
        units = [(a, e) for a in range(nt) for e in heads]
        s0 = pl.multiple_of(taus[-1] * tq, tq)
        kt = k_ref[0, pl.ds(s0, nt * tq), :]
        vt = v_ref[0, pl.ds(s0, nt * tq), :]
        zz = [lax.dot_general(qm[e], kt, _NT, preferred_element_type=f32) for e in heads]
        z = {(a, e): zz[e][:, (nt - 1 - a) * tq:(nt - a) * tq] + key_bias[a][e] for a, e in units}
        sp = {u: _softplus(z[u]) for u in units}
        hi = {u: sp[u].astype(bf16) for u in units}
        lo_ = {u: (sp[u] - hi[u].astype(f32)).astype(bf16) for u in units}
        r = {u: [jnp.dot(jnp.concatenate([hi[u][:, j * sub:(j + 1) * sub], lo_[u][:, j * sub:(j + 1) * sub]], axis=1),
                         tab_ref[...], preferred_element_type=f32) for j in range(nsub)] for u in units}
        t = {}
        for e in heads:
            c = carry_ref[e]
            for a in range(len(taus)):
                after = [None] * nsub
                for j in reversed(range(nsub)):
                    after[j] = r[a, e][j][:, :sub] + c
                    c = c + r[a, e][j][:, sub:]
                t[a, e] = z[a, e] + jnp.concatenate(after, axis=1)
            carry_ref[e] = c
        w = {u: jnp.exp(t[u]).astype(bf16) for u in units}
        for e in heads:
            we = jnp.concatenate([w[a, e] for a in reversed(range(nt))], axis=1) if nt > 1 else w[0, e]
            acc_ref[e] += jnp.dot(we, vt, preferred_element_type=f32)

    def row_bias(tau):
        spos = tau * tq + lax.broadcasted_iota(jnp.int32, (1, tq), 1)
        return [jnp.where(spos >= lo, b, -jnp.inf) for b in bias]

    row = lax.broadcasted_iota(jnp.int32, (tq, tq), 0)
    col = lax.broadcasted_iota(jnp.int32, (tq, tq), 1)
    readable = (col < row) & (i * tq + col >= lo)
    sweep([i], [[jnp.where(readable, b, -jnp.inf) for b in bias]])

    @pl.when(i % 2 == 1)
    def _():
        sweep([i - 1], [row_bias(i - 1)])

    def body(n, carry):
        tau = i - 1 - i % 2 - 2 * n
        sweep([tau, tau - 1], [row_bias(tau), row_bias(tau - 1)])
        return carry

    lax.fori_loop(0, i // 2, body, 0)
    out = acc_ref[0]
    for e in range(1, SB_HEADS):
        out = jnp.where(lane // DH_B == e, acc_ref[e], out)
    o_ref[0] = out.astype(o_ref.dtype)


def _sb_prompt(q, k, v, bias, lo):
    nb, length, width = q.shape
    nblk = length // BLOCK
    tq = BLOCK * max(d for d in (1, 2, 3, 4) if nblk % d == 0)
    lw = SB_HEADS * DH_B
    blk = pl.BlockSpec((1, tq, lw), lambda b, hg, i: (b, i, hg))
    full = pl.BlockSpec((1, length, lw), lambda b, hg, i: (b, 0, hg))
    tab = _neg_cumsum_table(BLOCK, inclusive=True)
    return pl.pallas_call(
        functools.partial(_sb_prompt_kernel, tq=tq, lo=lo),
        grid=(nb, width // lw, length // tq),
        in_specs=[pl.BlockSpec(memory_space=pltpu.SMEM), blk, full, full,
                  pl.BlockSpec(tab.shape, lambda b, hg, i: (0, 0))],
        out_specs=blk,
        out_shape=jax.ShapeDtypeStruct((nb, length, width), bf16),
        scratch_shapes=[pltpu.VMEM((SB_HEADS, tq, BLOCK), f32), pltpu.VMEM((SB_HEADS, tq, lw), f32)],
        compiler_params=_params("parallel", "parallel", "arbitrary"),
        name="sb_prompt",
    )(bias, q, k, v, tab)


QH = 32


def _sb_sample_kernel(pt_ref, q_ref, kn_ref, vn_ref, bias_ref, tab_ref, *rest, n_pg, n_grp, n_tok):
    k_refs = rest[:n_pg]
    v_refs = rest[n_pg:2 * n_pg]
    o_ref, carry_sc, acc_sc = rest[2 * n_pg:]
    g = pl.program_id(1)
    width = H_B * DH_B
    psz = tab_ref.shape[0] // 2
    row_head = lax.broadcasted_iota(jnp.int32, (QH, width), 0) % H_B
    lane_head = lax.broadcasted_iota(jnp.int32, (QH, width), 1) // DH_B
    q = q_ref[0]
    qrows = jnp.concatenate([jnp.broadcast_to(q[t:t + 1, :], (H_B, width)) for t in range(n_tok)], axis=0)
    qcat = jnp.where(lane_head == row_head, qrows, jnp.zeros_like(qrows)).astype(bf16)

    def weights(zs):
        n = len(zs)
        z = jnp.concatenate(zs, axis=0) if n > 1 else zs[0]
        sp = _softplus(z)
        hi = sp.astype(bf16)
        lo_ = (sp - hi.astype(f32)).astype(bf16)
        r = jnp.dot(jnp.concatenate([hi, lo_], axis=1), tab_ref[...], preferred_element_type=f32)
        c = carry_sc[...]
        after = [None] * n
        for p in reversed(range(n)):
            after[p] = r[p * QH:(p + 1) * QH, :psz] + c
            c = c + r[p * QH:(p + 1) * QH, psz:]
        carry_sc[...] = c
        w = jnp.exp(z - sp + (jnp.concatenate(after, axis=0) if n > 1 else after[0])).astype(bf16)
        return [w[p * QH:(p + 1) * QH, :] for p in range(n)]

    @pl.when(g == 0)
    def _():
        carry_sc[...] = jnp.zeros_like(carry_sc)
        pad = jnp.zeros((psz - kn_ref.shape[1], width), f32)
        kn = jnp.concatenate([kn_ref[0], pad], axis=0).astype(bf16)
        vn = jnp.concatenate([vn_ref[0], pad], axis=0).astype(bf16)
        j = lax.broadcasted_iota(jnp.int32, (QH, psz), 1)
        t = lax.broadcasted_iota(jnp.int32, (QH, psz), 0) // H_B
        z = lax.dot_general(qcat, kn, _NT, preferred_element_type=f32) + jnp.where(j < t, bias_ref[...], -jnp.inf)
        acc_sc[...] = jnp.dot(weights([z])[0], vn, preferred_element_type=f32)

    ws = weights([jnp.dot(qcat, k_refs[p][0, 0].astype(bf16), preferred_element_type=f32) + bias_ref[...]
                  for p in range(n_pg)])
    acc = acc_sc[...]
    for p in range(n_pg):
        acc = acc + lax.dot_general(ws[p], v_refs[p][0, 0].astype(bf16), _NT, preferred_element_type=f32)
    acc_sc[...] = acc

    @pl.when(g == n_grp - 1)
    def _():
        own = jnp.where(lane_head == row_head, acc_sc[...], 0.0)
        rows = [jnp.sum(own[t * H_B:(t + 1) * H_B, :], axis=0, keepdims=True) for t in range(n_tok)]
        rows.append(jnp.zeros((o_ref.shape[1] - n_tok, width), f32))
        o_ref[0] = jnp.concatenate(rows, axis=0).astype(o_ref.dtype)


def _sb_sample(q, kn, vn, bias, cache_k, cache_v, page_table, layer, n_tok, n_pg=32):
    db, rows, width = q.shape
    n_pages = page_table.shape[1]
    n_layers, n_pool, psz = cache_k.shape[:3]
    n_grp = n_pages // n_pg
    assert n_tok * H_B == QH and n_pages % n_pg == 0
    tab = _neg_cumsum_table(psz)
    cache_k = jnp.transpose(cache_k, (0, 1, 3, 4, 2)).reshape(n_layers, n_pool, width, psz)
    cache_v = jnp.transpose(cache_v, (0, 1, 3, 4, 2)).reshape(n_layers, n_pool, width, psz)

    def page_spec(p):
        return pl.BlockSpec((1, 1, width, psz),
                            lambda b, g, pt: (layer, pt[b, (n_grp - 1 - g) * n_pg + p], 0, 0))

    small = pl.BlockSpec((1, rows, width), lambda b, g, pt: (b, 0, 0))
    grid_spec = pltpu.PrefetchScalarGridSpec(
        num_scalar_prefetch=1,
        grid=(db, n_grp),
        in_specs=[small, small, small,
                  pl.BlockSpec(bias.shape, lambda b, g, pt: (0, 0)),
                  pl.BlockSpec(tab.shape, lambda b, g, pt: (0, 0))]
        + [page_spec(p) for p in range(n_pg)] * 2,
        out_specs=small,
        scratch_shapes=[pltpu.VMEM((QH, psz), f32), pltpu.VMEM((QH, width), f32)])
    return pl.pallas_call(
        functools.partial(_sb_sample_kernel, n_pg=n_pg, n_grp=n_grp, n_tok=n_tok),
        grid_spec=grid_spec,
        out_shape=jax.ShapeDtypeStruct((db, rows, width), bf16),
        compiler_params=_params("parallel", "arbitrary"),
        name="sb_sample",
    )(page_table, q, kn, vn, bias, tab, *([cache_k] * n_pg), *([cache_v] * n_pg))


GLA_CHUNKS_PER_STEP = 2


def _gla_tables(c):
    t = np.arange(c)[:, None]
    j = np.arange(c)[None, :]
    mats = [(j <= t), (j > t)]
    s = 1
    while s < c:
        bound = (t // (2 * s)) * (2 * s) + s - 1
        right = (t // s) % 2 == 1
        mats.append(np.where(right, (j > bound) & (j <= t), (j > t) & (j <= bound)))
        s *= 2
    return jnp.asarray(np.concatenate(mats, axis=0).astype(np.float32), bf16)


def _gla_kernel(q_ref, k_ref, v_ref, r_ref, gd_ref, wg_ref, bg_ref, e_ref, s0_ref, gh_ref,
                y_ref, S_ref, *, c, nck, lo, hi):
    j = pl.program_id(1)

    @pl.when(j == 0)
    def _():
        S_ref[...] = s0_ref[...]


    rows = nck * c
    pos = j * rows + lax.broadcasted_iota(jnp.int32, (rows, 1), 0)
    valid = (pos >= lo) & (pos < hi)
    rowc = lax.broadcasted_iota(jnp.int32, (c, 1), 0)
    row = lax.broadcasted_iota(jnp.int32, (c, c), 0)
    col = lax.broadcasted_iota(jnp.int32, (c, c), 1)
    lg = _log_sigmoid(jnp.dot(gd_ref[0].astype(bf16), wg_ref[...], preferred_element_type=f32)
                      + bg_ref[...]) / GATE_TAU
    lg = jnp.where(valid, lg, 0.0)
    n_lvl = e_ref.shape[0] // c - 2

    heads = range(H_C)
    items = [(h, u) for h in heads for u in range(nck)]
    sk = {(h, u): (slice(u * c, (u + 1) * c), slice(h * DK_C, (h + 1) * DK_C)) for h, u in items}
    sv = {(h, u): (slice(u * c, (u + 1) * c), slice(h * DV_C, (h + 1) * DV_C)) for h, u in items}
    q = {i: q_ref[0, sk[i][0], sk[i][1]] * (DK_C ** -0.5) for i in items}
    k = {i: jnp.where(valid[sk[i][0], :], k_ref[0, sk[i][0], sk[i][1]], 0.0) for i in items}
    vb = {i: v_ref[0, sv[i][0], sv[i][1]].astype(bf16) for i in items}
    x = {i: _dot01_left(e_ref[...], lg[sk[i][0], sk[i][1]], terms=2) for i in items}
    fx = {i: jnp.exp(x[i]) for i in items}
    kv = {i: lax.dot_general((k[i] * fx[i][c:2 * c, :]).astype(bf16), vb[i], _TN, preferred_element_type=f32)
          for i in items}
    a = {i: jnp.where(row == col, jnp.sum(q[i] * k[i], axis=1, keepdims=True), 0.0) for i in items}
    for lvl in range(n_lvl):
        s = 1 << lvl
        right = (rowc // s) % 2 == 1
        same = row // (2 * s) == col // (2 * s)
        fl = {i: fx[i][(2 + lvl) * c:(3 + lvl) * c, :] for i in items}
        qt = {i: jnp.where(right, q[i] * fl[i], 0.0).astype(bf16) for i in items}
        kt = {i: jnp.where(right, 0.0, k[i] * fl[i]).astype(bf16) for i in items}
        p = {i: lax.dot_general(qt[i], kt[i], _NT, preferred_element_type=f32) for i in items}
        a = {i: a[i] + jnp.where(same, p[i], 0.0) for i in items}
    o = {i: jnp.dot(a[i].astype(bf16), vb[i], preferred_element_type=f32) for i in items}
    for h in heads:
        S = S_ref[0, h]
        for u in range(nck):
            i = (h, u)
            o[i] = o[i] + jnp.dot((q[i] * fx[i][0:c, :]).astype(bf16), S.astype(bf16), preferred_element_type=f32)
            b_last = x[i][c - 1:c, :]
            decay = jnp.exp(jnp.transpose(jnp.broadcast_to(b_last, (DK_C, DK_C))))
            S = jnp.concatenate([decay] * (DV_C // DK_C), axis=1) * S + kv[i]
        S_ref[0, h] = S
    for i in items:
        y = o[i] * lax.rsqrt(jnp.mean(o[i] * o[i], axis=1, keepdims=True) + EPS) * gh_ref[:, sv[i][1]]
        y_ref[0, sv[i][0], sv[i][1]] = (y * jax.nn.silu(r_ref[0, sv[i][0], sv[i][1]])).astype(y_ref.dtype)


def _gla(q, k, v, r, gd, wg, bg, s0, gh, lo, hi):
    nb, length, _ = q.shape
    c = CHUNK_C
    nck = GLA_CHUNKS_PER_STEP
    e = _gla_tables(c)
    blk = lambda w: pl.BlockSpec((1, nck * c, w), lambda b, j: (b, j, 0))
    const = lambda a: pl.BlockSpec(a.shape, lambda b, j: (0,) * a.ndim)
    st = pl.BlockSpec((1, H_C, DK_C, DV_C), lambda b, j: (b, 0, 0, 0))
    return pl.pallas_call(
        functools.partial(_gla_kernel, c=c, nck=nck, lo=lo, hi=hi),
        grid=(nb, length // (nck * c)),
        in_specs=[blk(H_C * DK_C), blk(H_C * DK_C), blk(H_C * DV_C), blk(H_C * DV_C), blk(LANES),
                  const(wg), const(bg), const(e), st, const(gh)],
        out_specs=[blk(H_C * DV_C), st],
        out_shape=[jax.ShapeDtypeStruct((nb, length, H_C * DV_C), bf16),
                   jax.ShapeDtypeStruct((nb, H_C, DK_C, DV_C), f32)],
        compiler_params=_params("parallel", "arbitrary"),
        name="gla",
    )(q, k, v, r, gd, wg, bg, e, s0, gh)


def _pad_rows(x, rows):
    return jnp.pad(x, ((0, 0), (0, rows - x.shape[1]), (0, 0)))


POST_ROW_TILES = (768, 512, 256, 128)
POST_FF_TILE = 1024


def _row_tile(m, candidates):
    for tm in candidates:
        if m % tm == 0:
            return tm
    raise ValueError(f"row count {m} is not a multiple of {candidates[-1]}")


def kernel(x_prompt, x_sample, cache_sb_k, cache_sb_v, state_mlstm_c, state_mlstm_n, state_mlstm_m, state_gla_s,
           page_table, meta_tokens, norm_gains, w_in_even, b_gate_even, b_sb_even, g_head_even, w_out_even,
           w_in_odd, w_gate_up_odd, b_gate_up_odd, g_head_odd, w_out_odd, w_up, w_down):
    B, seq, D = x_prompt.shape
    DB, T, _ = x_sample.shape
    depth = norm_gains.shape[0]
    n_meta = meta_tokens.shape[0]
    meta_pad = BLOCK - n_meta
    Lp = meta_pad + n_meta + seq
    assert Lp % BLOCK == 0 and (DB * T) % LANES == 0 and T <= SUBLANES

    hp = jnp.concatenate([jnp.zeros((B, meta_pad, D), f32),
                          jnp.broadcast_to(meta_tokens[None], (B, n_meta, D)), x_prompt], axis=1)
    hp = hp.reshape(B * Lp, D)
    hs = x_sample.reshape(DB * T, D)
    tm_p, tm_s = _row_tile(B * Lp, POST_ROW_TILES), _row_tile(DB * T, POST_ROW_TILES)
    tm_in = _row_tile(Lp, (384, 256, 128))

    na, nb_ = H_A * DK_A, H_B * DH_B
    nk, nv = H_C * DK_C, H_C * DV_C
    o_qa, o_ka, o_va, o_oa = 0, na, 2 * na, 3 * na
    o_qb, o_kb, o_vb, o_g = 4 * na, 4 * na + nb_, 4 * na + 2 * nb_, 4 * na + 3 * nb_
    even_common = (_Out(o_qa, na), _Out(o_ka, na), _Out(o_va, na), _Out(o_oa, na), _Out(o_g, LANES),
                   _Out(o_qb, nb_, bf16, DH_B ** -0.5), _Out(o_kb, nb_, bf16), _Out(o_vb, nb_, bf16))
    even_outs_p = even_common + (_Out(o_kb, nb_, transposed=True), _Out(o_vb, nb_, transposed=True))
    even_outs_s = even_common + (_Out(o_kb, nb_), _Out(o_vb, nb_))
    odd_outs = (_Out(0, nk), _Out(nk, nk), _Out(2 * nk, nv), _Out(2 * nk + nv, nv), _Out(2 * nk + 2 * nv, LANES))

    outs = {n: [] for n in ("pk", "pv", "pc", "pn", "pm", "ps", "sk", "sv", "sc", "sn", "sm", "ss")}

    for layer in range(depth):
        gains = norm_gains[layer]
        if layer % 2 == 0:
            e = layer // 2
            w = w_in_even[e]
            w = jnp.concatenate([w[:, :4 * na], w[:, 4 * na + 2 * H_A:], w[:, 4 * na:4 * na + 2 * H_A],
                                 jnp.zeros((D, LANES - 2 * H_A), f32)], axis=1).astype(bf16)
            brow = jnp.pad(b_gate_even[e], (0, LANES - 2 * H_A)).reshape(1, LANES)
            bcol = b_gate_even[e].reshape(2 * H_A, 1)
            gh = g_head_even[e].reshape(1, -1)
            w_out = w_out_even[e].astype(bf16)

            qa, ka, va, oa, gt, qb16, kb16, vb16, kbt, vbt = _in_proj(hp, gains[0:1], w, even_outs_p, tm_in, Lp)
            r3 = lambda a: a.reshape(B, Lp, a.shape[-1])
            g3 = r3(gt)
            ya, C, n, m = _mlstm(r3(qa), r3(ka), r3(va), r3(oa), g3, jnp.swapaxes(g3[:, :, :2 * H_A], 1, 2),
                                 brow, bcol, jnp.zeros((B, H_A, DK_A, DV_A), f32), jnp.zeros((B, H_A, DK_A), f32),
                                 jnp.zeros((B, H_A, 1), f32), gh, meta_pad, Lp)
            hb = _sb_prompt(r3(qb16), r3(kb16), r3(vb16), b_sb_even[e], meta_pad)
            hp = _post([ya.reshape(B * Lp, -1), hb.reshape(B * Lp, -1)], hp, w_out, gains,
                       w_up[layer].astype(bf16), w_down[layer].astype(bf16), tm_p, POST_FF_TILE)
            rows = lambda t: jnp.transpose(t.reshape(B, H_B, DH_B, Lp)[..., meta_pad:], (0, 3, 1, 2))
            outs["pk"].append(rows(kbt))
            outs["pv"].append(rows(vbt))
            outs["pc"].append(C)
            outs["pn"].append(n)
            outs["pm"].append(m.reshape(B, H_A))

            qa, ka, va, oa, gt, qb16, kb16, vb16, kb, vb = _in_proj(hs, gains[0:1], w, even_outs_s, tm_s)
            s3 = lambda a: a.reshape(DB, T, a.shape[-1])
            pc = lambda a: _pad_rows(s3(a), CHUNK_A)
            g3 = pc(gt)
            ya, C, n, m = _mlstm(pc(qa), pc(ka), pc(va), pc(oa), g3, jnp.swapaxes(g3[:, :, :2 * H_A], 1, 2),
                                 brow, bcol, state_mlstm_c[e], state_mlstm_n[e],
                                 state_mlstm_m[e].reshape(DB, H_A, 1), gh, 0, T)
            p8 = lambda a: _pad_rows(s3(a), SUBLANES)
            bias_rows = jnp.broadcast_to(jnp.tile(b_sb_even[e], T)[:, None], (T * H_B, cache_sb_k.shape[2]))
            hb = _sb_sample(p8(qb16.astype(f32)), p8(kb), p8(vb), bias_rows, cache_sb_k, cache_sb_v, page_table, e, T)
            hs = _post([ya[:, :T].reshape(DB * T, -1), hb[:, :T].reshape(DB * T, -1)], hs, w_out, gains,
                       w_up[layer].astype(bf16), w_down[layer].astype(bf16), tm_s, POST_FF_TILE)
            outs["sk"].append(kb.reshape(DB, T, H_B, DH_B))
            outs["sv"].append(vb.reshape(DB, T, H_B, DH_B))
            outs["sc"].append(C)
            outs["sn"].append(n)
            outs["sm"].append(m.reshape(DB, H_A))
        else:
            o = layer // 2
            w = jnp.concatenate([w_in_odd[o], jnp.zeros((D, LANES - GATE_RANK), f32)], axis=1).astype(bf16)
            wg = jnp.concatenate([w_gate_up_odd[o], jnp.zeros((LANES - GATE_RANK, nk), f32)], axis=0).astype(bf16)
            bg = b_gate_up_odd[o].reshape(1, nk)
            gh = g_head_odd[o].reshape(1, -1)
            w_out = w_out_odd[o].astype(bf16)

            q, k, v, r, gd = _in_proj(hp, gains[0:1], w, odd_outs, tm_in)
            r3 = lambda a: a.reshape(B, Lp, a.shape[-1])
            y, S = _gla(r3(q), r3(k), r3(v), r3(r), r3(gd), wg, bg, jnp.zeros((B, H_C, DK_C, DV_C), f32), gh,
                        meta_pad, Lp)
            hp = _post([y.reshape(B * Lp, -1)], hp, w_out, gains,
                       w_up[layer].astype(bf16), w_down[layer].astype(bf16), tm_p, POST_FF_TILE)
            outs["ps"].append(S)

            q, k, v, r, gd = _in_proj(hs, gains[0:1], w, odd_outs, tm_s)
            pc = lambda a: _pad_rows(a.reshape(DB, T, a.shape[-1]), GLA_CHUNKS_PER_STEP * CHUNK_C)
            y, S = _gla(pc(q), pc(k), pc(v), pc(r), pc(gd), wg, bg, state_gla_s[o], gh, 0, T)
            hs = _post([y[:, :T].reshape(DB * T, -1)], hs, w_out, gains,
                       w_up[layer].astype(bf16), w_down[layer].astype(bf16), tm_s, POST_FF_TILE)
            outs["ss"].append(S)

    y_prompt = hp.reshape(B, Lp, D)[:, meta_pad + n_meta:]
    y_sample = hs.reshape(DB, T, D)
    st = jnp.stack
    return (y_prompt, y_sample, st(outs["pk"]), st(outs["pv"]), st(outs["pc"]), st(outs["pn"]), st(outs["pm"]),
            st(outs["ps"]), st(outs["sk"]), st(outs["sv"]), st(outs["sc"]), st(outs["sn"]), st(outs["sm"]),
            st(outs["ss"]))
```

```python
import functools
from typing import Any, NamedTuple

import jax
import jax.numpy as jnp
import numpy as np
from jax import lax
from jax.experimental import pallas as pl
from jax.experimental.pallas import tpu as pltpu

f32 = jnp.float32
bf16 = jnp.bfloat16

BLOCK = 128
H_A, DK_A, DV_A = 4, 128, 128
H_B, DH_B = 8, 64
H_C, DK_C, DV_C = 4, 128, 256
GATE_RANK = 16
GATE_TAU = 16.0
CHUNK_A = 128
CHUNK_C = 64
EPS = 1e-6
LOG2E = 1.4426950408889634

LANES = 128
SUBLANES = 8
VMEM_LIMIT = 48 * 1024 * 1024

_NT = (((1,), (1,)), ((), ()))
_TN = (((0,), (0,)), ((), ()))


def _rms(x, g):
    return x * lax.rsqrt(jnp.mean(x * x, axis=-1, keepdims=True) + EPS) * g


def _softplus(z):
    return jnp.maximum(z, 0.0) + jnp.log(1.0 + jnp.exp2(jnp.abs(z) * (-LOG2E)))


def _log_sigmoid(z):
    return jnp.minimum(z, 0.0) - jnp.log1p(jnp.exp(-jnp.abs(z)))


def _split(x, terms):
    parts = []
    for _ in range(terms - 1):
        p = x.astype(bf16)
        parts.append(p)
        x = x - p.astype(f32)
    parts.append(x.astype(bf16))
    return parts


def _dot01_left(m01, x, terms=3):
    return sum(jnp.dot(m01, p, preferred_element_type=f32) for p in _split(x, terms))


def _dot01_right(x, m01, terms=3):
    return sum(jnp.dot(p, m01, preferred_element_type=f32) for p in _split(x, terms))


def _params(*sem):
    return pltpu.CompilerParams(dimension_semantics=sem, vmem_limit_bytes=VMEM_LIMIT)


class _Out(NamedTuple):
    off: int
    n: int
    dtype: Any = f32
    scale: float = 1.0
    transposed: bool = False


def _in_proj_kernel(x_ref, g_ref, w_ref, *o_refs, outs):
    xn = _rms(x_ref[...], g_ref[...]).astype(bf16)
    cache = {}
    for o_ref, out in zip(o_refs, outs):
        if (out.off, out.n) not in cache:
            cache[(out.off, out.n)] = jnp.dot(xn, w_ref[:, out.off:out.off + out.n], preferred_element_type=f32)
        y = cache[(out.off, out.n)]
        if out.scale != 1.0:
            y = y * out.scale
        if out.transposed:
            o_ref[0] = y.T.astype(o_ref.dtype)
        else:
            o_ref[...] = y.astype(o_ref.dtype)


def _in_proj(x, g, w, outs, tm, seq_len=None):
    m, d = x.shape
    n_tot = w.shape[1]
    seq_len = seq_len or m
    tiles = seq_len // tm
    assert seq_len % tm == 0 and m % seq_len == 0
    return pl.pallas_call(
        functools.partial(_in_proj_kernel, outs=outs),
        grid=(m // tm,),
        in_specs=[pl.BlockSpec((tm, d), lambda i: (i, 0)),
                  pl.BlockSpec((1, d), lambda i: (0, 0)),
                  pl.BlockSpec((d, n_tot), lambda i: (0, 0))],
        out_specs=[pl.BlockSpec((1, o.n, tm), lambda i: (i // tiles, 0, i % tiles)) if o.transposed
                   else pl.BlockSpec((tm, o.n), lambda i: (i, 0)) for o in outs],
        out_shape=[jax.ShapeDtypeStruct((m // seq_len, o.n, seq_len) if o.transposed else (m, o.n), o.dtype)
                   for o in outs],
        compiler_params=_params("parallel"),
        name="in_proj",
    )(x, g, w)


def _post_kernel(*refs, n_a, nf):
    a_refs = refs[:n_a]
    h_ref, wo_ref, g_ref, wu_ref, wd_ref, out_ref, h1_sc, xn_sc, acc_sc = refs[n_a:]
    f = pl.program_id(1)

    @pl.when(f == 0)
    def _():
        if n_a > 1:
            a = jnp.concatenate([r[...] for r in a_refs], axis=-1)
        else:
            a = a_refs[0][...]
        mix = jnp.dot(a, wo_ref[...], preferred_element_type=f32)
        h1 = h_ref[...] + _rms(mix, g_ref[1:2, :])
        h1_sc[...] = h1
        xn_sc[...] = _rms(h1, g_ref[2:3, :]).astype(bf16)
        acc_sc[...] = jnp.zeros_like(acc_sc)

    u = jnp.dot(xn_sc[...], wu_ref[...], preferred_element_type=f32)
    u = jnp.maximum(u, 0.0)
    acc_sc[...] += jnp.dot((u * u).astype(bf16), wd_ref[...], preferred_element_type=f32)

    @pl.when(f == nf - 1)
    def _():
        out_ref[...] = h1_sc[...] + _rms(acc_sc[...], g_ref[3:4, :])


def _post(a_list, h, w_out, gains, w_up, w_down, tm, tf):
    m, d = h.shape
    dff = w_up.shape[1]
    nf = dff // tf
    n_a = len(a_list)
    return pl.pallas_call(
        functools.partial(_post_kernel, n_a=n_a, nf=nf),
        grid=(m // tm, nf),
        in_specs=[pl.BlockSpec((tm, a.shape[1]), lambda i, f: (i, 0)) for a in a_list] + [
            pl.BlockSpec((tm, d), lambda i, f: (i, 0)),
            pl.BlockSpec(w_out.shape, lambda i, f: (0, 0)),
            pl.BlockSpec(gains.shape, lambda i, f: (0, 0)),
            pl.BlockSpec((d, tf), lambda i, f: (0, f)),
            pl.BlockSpec((tf, d), lambda i, f: (f, 0))],
        out_specs=pl.BlockSpec((tm, d), lambda i, f: (i, 0)),
        out_shape=jax.ShapeDtypeStruct((m, d), f32),
        scratch_shapes=[pltpu.VMEM((tm, d), f32), pltpu.VMEM((tm, d), bf16), pltpu.VMEM((tm, d), f32)],
        compiler_params=_params("parallel", "arbitrary"),
        name="post",
    )(*a_list, h, w_out, gains, w_up, w_down)


MLSTM_SEQS_PER_STEP = 2

def _mlstm_kernel(q_ref, k_ref, v_ref, o_ref, g_ref, gt_ref, brow_ref, bcol_ref, c0_ref, n0_ref, m0_ref, gh_ref,
                  y_ref, C_ref, n_ref, m_ref, *, c, lo, hi):
    j = pl.program_id(1)

    @pl.when(j == 0)
    def _():
        C_ref[...] = c0_ref[...]
        n_ref[...] = n0_ref[...]
        m_ref[...] = m0_ref[...]

    row = lax.broadcasted_iota(jnp.int32, (c, c), 0)
    col = lax.broadcasted_iota(jnp.int32, (c, c), 1)
    causal = row >= col
    tril = jnp.where(causal, 1.0, 0.0).astype(bf16)
    triu = jnp.where(row <= col, 1.0, 0.0).astype(bf16)
    pos_c = j * c + lax.broadcasted_iota(jnp.int32, (c, 1), 0)
    valid_c = (pos_c >= lo) & (pos_c < hi)
    pos_r = j * c + lax.broadcasted_iota(jnp.int32, (1, c), 1)
    valid_r = (pos_r >= lo) & (pos_r < hi)

    seqs = range(q_ref.shape[0])
    gc = [g_ref[s] + brow_ref[...] for s in seqs]
    li_c = [jnp.where(valid_c, gc[s], -jnp.inf) for s in seqs]
    lf_c = [jnp.where(valid_c, _log_sigmoid(gc[s]), 0.0) for s in seqs]
    b_c = [_dot01_left(tril, lf_c[s]) for s in seqs]
    gr = [gt_ref[s] + bcol_ref[...] for s in seqs]
    li_r = [jnp.where(valid_r, gr[s], -jnp.inf) for s in seqs]
    lf_r = [jnp.where(valid_r, _log_sigmoid(gr[s]), 0.0) for s in seqs]
    b_r = [_dot01_right(lf_r[s], triu) for s in seqs]

    units = [(s, h) for s in seqs for h in range(H_A)]
    sl = {u: slice(u[1] * DK_A, (u[1] + 1) * DK_A) for u in units}
    q = {u: q_ref[u[0], :, sl[u]] for u in units}
    k = {u: k_ref[u[0], :, sl[u]] * (DK_A ** -0.5) for u in units}
    qb = {u: q[u].astype(bf16) for u in units}
    kb = {u: k[u].astype(bf16) for u in units}
    vb = {u: v_ref[u[0], :, sl[u]].astype(bf16) for u in units}
    bc = {(s, h): b_c[s][:, H_A + h:H_A + h + 1] for s, h in units}
    ic = {(s, h): li_c[s][:, h:h + 1] for s, h in units}
    br = {(s, h): b_r[s][H_A + h:H_A + h + 1, :] for s, h in units}
    ir = {(s, h): li_r[s][h:h + 1, :] for s, h in units}
    m = {(s, h): m_ref[s, h:h + 1, :] for s, h in units}
    C = {(s, h): C_ref[s, h] for s, h in units}
    n = {(s, h): n_ref[s, h:h + 1, :] for s, h in units}

    qk = {u: lax.dot_general(qb[u], kb[u], _NT, preferred_element_type=f32) for u in units}
    qc = {u: jnp.dot(qb[u], C[u].astype(bf16), preferred_element_type=f32) for u in units}
    d_log = {u: jnp.where(causal, bc[u] - br[u] + ir[u], -jnp.inf) for u in units}
    inter = {u: bc[u] + m[u] for u in units}
    m_t = {u: jnp.maximum(inter[u], jnp.max(d_log[u], axis=1, keepdims=True)) for u in units}
    sc = {u: qk[u] * jnp.exp(d_log[u] - m_t[u]) for u in units}
    sv = {u: jnp.dot(sc[u].astype(bf16), vb[u], preferred_element_type=f32) for u in units}
    w_i = {u: jnp.exp(inter[u] - m_t[u]) for u in units}
    den = {u: w_i[u] * jnp.sum(q[u] * n[u], axis=1, keepdims=True) + jnp.sum(sc[u], axis=1, keepdims=True)
           for u in units}

    b_last = {u: bc[u][c - 1:c, :] for u in units}
    gcol = {u: b_last[u] - bc[u] + ic[u] for u in units}
    m_new = {u: jnp.maximum(b_last[u] + m[u], jnp.max(gcol[u], axis=0, keepdims=True)) for u in units}
    w_d = {u: jnp.exp(b_last[u] + m[u] - m_new[u]) for u in units}
    kw = {u: k[u] * jnp.exp(gcol[u] - m_new[u]) for u in units}
    kv = {u: lax.dot_general(kw[u].astype(bf16), vb[u], _TN, preferred_element_type=f32) for u in units}
    for s, h in units:
        u = (s, h)
        C_ref[s, h] = w_d[u] * C[u] + kv[u]
        n_ref[s, h:h + 1, :] = w_d[u] * n[u] + jnp.sum(kw[u], axis=0, keepdims=True)
        m_ref[s, h:h + 1, :] = m_new[u]

    for s, h in units:
        u = (s, h)
        hh = (w_i[u] * qc[u] + sv[u]) / jnp.maximum(jnp.abs(den[u]), jnp.exp(-m_t[u]))
        y = hh * lax.rsqrt(jnp.mean(hh * hh, axis=1, keepdims=True) + EPS) * gh_ref[:, sl[u]]
        y_ref[s, :, sl[u]] = (y * jax.nn.sigmoid(o_ref[s, :, sl[u]])).astype(y_ref.dtype)


def _mlstm(q, k, v, o, g, gt, brow, bcol, c0, n0, m0, gh, lo, hi):
    nb, length, _ = q.shape
    c = CHUNK_A
    ns = MLSTM_SEQS_PER_STEP if nb % MLSTM_SEQS_PER_STEP == 0 else 1
    wide = pl.BlockSpec((ns, c, H_A * DK_A), lambda b, j: (b, j, 0))
    st = lambda shape: pl.BlockSpec((ns,) + shape, lambda b, j: (b,) + (0,) * len(shape))
    const = lambda a: pl.BlockSpec(a.shape, lambda b, j: (0,) * a.ndim)
    return pl.pallas_call(
        functools.partial(_mlstm_kernel, c=c, lo=lo, hi=hi),
        grid=(nb // ns, length // c),
        in_specs=[wide, wide, wide, wide,
                  pl.BlockSpec((ns, c, LANES), lambda b, j: (b, j, 0)),
                  pl.BlockSpec((ns, 2 * H_A, c), lambda b, j: (b, 0, j)),
                  const(brow), const(bcol),
                  st((H_A, DK_A, DV_A)), st((H_A, DK_A)), st((H_A, 1)), const(gh)],
        out_specs=[wide, st((H_A, DK_A, DV_A)), st((H_A, DK_A)), st((H_A, 1))],
        out_shape=[jax.ShapeDtypeStruct((nb, length, H_A * DV_A), bf16),
                   jax.ShapeDtypeStruct((nb, H_A, DK_A, DV_A), f32),
                   jax.ShapeDtypeStruct((nb, H_A, DK_A), f32),
                   jax.ShapeDtypeStruct((nb, H_A, 1), f32)],
        compiler_params=_params("parallel", "arbitrary"),
        name="mlstm",
    )(q, k, v, o, g, gt, brow, bcol, c0, n0, m0, gh)


SB_HEADS = 4


def _neg_cumsum_table(n, inclusive=False):
    j = np.arange(n)[:, None]
    s = np.arange(n)[None, :]
    later = (j >= s) if inclusive else (j > s)
    half = np.concatenate([later.astype(np.float32), np.ones((n, n), np.float32)], axis=1)
    return jnp.asarray(-np.concatenate([half, half], axis=0), bf16)


def _sb_prompt_kernel(bias_ref, q_ref, k_ref, v_ref, tab_ref, o_ref, carry_ref, acc_ref, *, tq, lo):
    hg = pl.program_id(1)
    i = pl.program_id(2)
    sub = tab_ref.shape[0] // 2
    lw = SB_HEADS * DH_B
    lane = lax.broadcasted_iota(jnp.int32, (tq, lw), 1)
    qg = q_ref[0]
    qm = [jnp.where(lane // DH_B == e, qg, jnp.zeros_like(qg)) for e in range(SB_HEADS)]
    bias = [bias_ref[SB_HEADS * hg + e] for e in range(SB_HEADS)]
    heads = range(SB_HEADS)
    carry_ref[...] = jnp.zeros_like(carry_ref)
    acc_ref[...] = jnp.zeros_like(acc_ref)

    def keys(ref, tau):
        return ref[0, pl.ds(pl.multiple_of(tau * tq, tq), tq), :]

    nsub = tq // sub

    def sweep(taus, key_bias):
        nt = len(taus)
        units = [(a, e) for a in range(nt) for e in heads]
        s0 = pl.multiple_of(taus[-1] * tq, tq)
        kt = k_ref[0, pl.ds(s0, nt * tq), :]
        vt = v_ref[0, pl.ds(s0, nt * tq), :]
        zz = [lax.dot_general(qm[e], kt, _NT, preferred_element_type=f32) for e in heads]
        z = {(a, e): zz[e][:, (nt - 1 - a) * tq:(nt - a) * tq] + key_bias[a][e] for a, e in units}
        sp = {u: _softplus(z[u]) for u in units}
        hi = {u: sp[u].astype(bf16) for u in units}
        lo_ = {u: (sp[u] - hi[u].astype(f32)).astype(bf16) for u in units}
        r = {u: [jnp.dot(jnp.concatenate([hi[u][:, j * sub:(j + 1) * sub], lo_[u][:, j * sub:(j + 1) * sub]], axis=1),
                         tab_ref[...], preferred_element_type=f32) for j in range(nsub)] for u in units}
        t = {}
        for e in heads:
            c = carry_ref[e]
            for a in range(len(taus)):
                after = [None] * nsub
                for j in reversed(range(nsub)):
                    after[j] = r[a, e][j][:, :sub] + c
                    c = c + r[a, e][j][:, sub:]
                t[a, e] = z[a, e] + jnp.concatenate(after, axis=1)
            carry_ref[e] = c
        w = {u: jnp.exp(t[u]).astype(bf16) for u in units}
        for e in heads:
            we = jnp.concatenate([w[a, e] for a in reversed(range(nt))], axis=1) if nt > 1 else w[0, e]
            acc_ref[e] += jnp.dot(we, vt, preferred_element_type=f32)

    def row_bias(tau):
        spos = tau * tq + lax.broadcasted_iota(jnp.int32, (1, tq), 1)
        return [jnp.where(spos >= lo, b, -jnp.inf) for b in bias]

    row = lax.broadcasted_iota(jnp.int32, (tq, tq), 0)
    col = lax.broadcasted_iota(jnp.int32, (tq, tq), 1)
    readable = (col < row) & (i * tq + col >= lo)
    sweep([i], [[jnp.where(readable, b, -jnp.inf) for b in bias]])

    @pl.when(i % 2 == 1)
    def _():
        sweep([i - 1], [row_bias(i - 1)])

    def body(n, carry):
        tau = i - 1 - i % 2 - 2 * n
        sweep([tau, tau - 1], [row_bias(tau), row_bias(tau - 1)])
        return carry

    lax.fori_loop(0, i // 2, body, 0)
    out = acc_ref[0]
    for e in range(1, SB_HEADS):
        out = jnp.where(lane // DH_B == e, acc_ref[e], out)
    o_ref[0] = out.astype(o_ref.dtype)


def _sb_prompt(q, k, v, bias, lo):
    nb, length, width = q.shape
    nblk = length // BLOCK
    tq = BLOCK * max(d for d in (1, 2, 3, 4) if nblk % d == 0)
    lw = SB_HEADS * DH_B
    blk = pl.BlockSpec((1, tq, lw), lambda b, hg, i: (b, i, hg))
    full = pl.BlockSpec((1, length, lw), lambda b, hg, i: (b, 0, hg))
    tab = _neg_cumsum_table(BLOCK, inclusive=True)
    return pl.pallas_call(
        functools.partial(_sb_prompt_kernel, tq=tq, lo=lo),
        grid=(nb, width // lw, length // tq),
        in_specs=[pl.BlockSpec(memory_space=pltpu.SMEM), blk, full, full,
                  pl.BlockSpec(tab.shape, lambda b, hg, i: (0, 0))],
        out_specs=blk,
        out_shape=jax.ShapeDtypeStruct((nb, length, width), bf16),
        scratch_shapes=[pltpu.VMEM((SB_HEADS, tq, BLOCK), f32), pltpu.VMEM((SB_HEADS, tq, lw), f32)],
        compiler_params=_params("parallel", "parallel", "arbitrary"),
        name="sb_prompt",
    )(bias, q, k, v, tab)


QH = 32
SB_PAGES_PER_STEP = 32


def _sb_sample_kernel(pt_ref, q_ref, kn_ref, vn_ref, bias_ref, tab_ref, *rest, n_pg, n_grp, n_tok):
    k_refs = rest[:n_pg]
    v_refs = rest[n_pg:2 * n_pg]
    o_ref, carry_sc, acc_sc = rest[2 * n_pg:]
    g = pl.program_id(1)
    width = H_B * DH_B
    psz = tab_ref.shape[0] // 2
    row_head = lax.broadcasted_iota(jnp.int32, (QH, width), 0) % H_B
    lane_head = lax.broadcasted_iota(jnp.int32, (QH, width), 1) // DH_B
    q = q_ref[0]
    qrows = jnp.concatenate([jnp.broadcast_to(q[t:t + 1, :], (H_B, width)) for t in range(n_tok)], axis=0)
    qcat = jnp.where(lane_head == row_head, qrows, jnp.zeros_like(qrows)).astype(bf16)

    def weights(zs):
        n = len(zs)
        z = jnp.concatenate(zs, axis=0) if n > 1 else zs[0]
        sp = _softplus(z)
        hi = sp.astype(bf16)
        lo_ = (sp - hi.astype(f32)).astype(bf16)
        r = jnp.dot(jnp.concatenate([hi, lo_], axis=1), tab_ref[...], preferred_element_type=f32)
        c = carry_sc[...]
        after = [None] * n
        for p in reversed(range(n)):
            after[p] = r[p * QH:(p + 1) * QH, :psz] + c
            c = c + r[p * QH:(p + 1) * QH, psz:]
        carry_sc[...] = c
        w = jnp.exp(z - sp + (jnp.concatenate(after, axis=0) if n > 1 else after[0])).astype(bf16)
        return [w[p * QH:(p + 1) * QH, :] for p in range(n)]

    @pl.when(g == 0)
    def _():
        carry_sc[...] = jnp.zeros_like(carry_sc)
        pad = jnp.zeros((psz - kn_ref.shape[1], width), f32)
        kn = jnp.concatenate([kn_ref[0], pad], axis=0).astype(bf16)
        vn = jnp.concatenate([vn_ref[0], pad], axis=0).astype(bf16)
        j = lax.broadcasted_iota(jnp.int32, (QH, psz), 1)
        t = lax.broadcasted_iota(jnp.int32, (QH, psz), 0) // H_B
        z = lax.dot_general(qcat, kn, _NT, preferred_element_type=f32) + jnp.where(j < t, bias_ref[...], -jnp.inf)
        acc_sc[...] = jnp.dot(weights([z])[0], vn, preferred_element_type=f32)

    ws = weights([jnp.dot(qcat, k_refs[p][0, 0].astype(bf16), preferred_element_type=f32) + bias_ref[...]
                  for p in range(n_pg)])
    acc = acc_sc[...]
    for p in range(n_pg):
        acc = acc + lax.dot_general(ws[p], v_refs[p][0, 0].astype(bf16), _NT, preferred_element_type=f32)
    acc_sc[...] = acc

    @pl.when(g == n_grp - 1)
    def _():
        own = jnp.where(lane_head == row_head, acc_sc[...], 0.0)
        rows = [jnp.sum(own[t * H_B:(t + 1) * H_B, :], axis=0, keepdims=True) for t in range(n_tok)]
        rows.append(jnp.zeros((o_ref.shape[1] - n_tok, width), f32))
        o_ref[0] = jnp.concatenate(rows, axis=0).astype(o_ref.dtype)


def _sb_sample(q, kn, vn, bias, cache_k, cache_v, page_table, layer, n_tok):
    db, rows, width = q.shape
    n_pages = page_table.shape[1]
    n_layers, n_pool, psz = cache_k.shape[:3]
    n_pg = next(d for d in range(min(SB_PAGES_PER_STEP, n_pages), 0, -1) if n_pages % d == 0)
    n_grp = n_pages // n_pg
    assert n_tok * H_B == QH
    tab = _neg_cumsum_table(psz)
    cache_k = jnp.transpose(cache_k, (0, 1, 3, 4, 2)).reshape(n_layers, n_pool, width, psz)
    cache_v = jnp.transpose(cache_v, (0, 1, 3, 4, 2)).reshape(n_layers, n_pool, width, psz)

    def page_spec(p):
        return pl.BlockSpec((1, 1, width, psz),
                            lambda b, g, pt: (layer, pt[b, (n_grp - 1 - g) * n_pg + p], 0, 0))

    small = pl.BlockSpec((1, rows, width), lambda b, g, pt: (b, 0, 0))
    grid_spec = pltpu.PrefetchScalarGridSpec(
        num_scalar_prefetch=1,
        grid=(db, n_grp),
        in_specs=[small, small, small,
                  pl.BlockSpec(bias.shape, lambda b, g, pt: (0, 0)),
                  pl.BlockSpec(tab.shape, lambda b, g, pt: (0, 0))]
        + [page_spec(p) for p in range(n_pg)] * 2,
        out_specs=small,
        scratch_shapes=[pltpu.VMEM((QH, psz), f32), pltpu.VMEM((QH, width), f32)])
    return pl.pallas_call(
        functools.partial(_sb_sample_kernel, n_pg=n_pg, n_grp=n_grp, n_tok=n_tok),
        grid_spec=grid_spec,
        out_shape=jax.ShapeDtypeStruct((db, rows, width), bf16),
        compiler_params=_params("parallel", "arbitrary"),
        name="sb_sample",
    )(page_table, q, kn, vn, bias, tab, *([cache_k] * n_pg), *([cache_v] * n_pg))


GLA_CHUNKS_PER_STEP = 2
GLA_SEQS_PER_STEP = 2


def _gla_tables(c):
    t = np.arange(c)[:, None]
    j = np.arange(c)[None, :]
    mats = [(j <= t), (j > t)]
    s = 1
    while s < c:
        bound = (t // (2 * s)) * (2 * s) + s - 1
        right = (t // s) % 2 == 1
        mats.append(np.where(right, (j > bound) & (j <= t), (j > t) & (j <= bound)))
        s *= 2
    return jnp.asarray(np.concatenate(mats, axis=0).astype(np.float32), bf16)


def _gla_kernel(q_ref, k_ref, v_ref, r_ref, gd_ref, wg_ref, bg_ref, e_ref, s0_ref, gh_ref,
                y_ref, S_ref, *, c, nck, lo, hi):
    j = pl.program_id(1)

    @pl.when(j == 0)
    def _():
        S_ref[...] = s0_ref[...]


    rows = nck * c
    pos = j * rows + lax.broadcasted_iota(jnp.int32, (rows, 1), 0)
    valid = (pos >= lo) & (pos < hi)
    rowc = lax.broadcasted_iota(jnp.int32, (c, 1), 0)
    row = lax.broadcasted_iota(jnp.int32, (c, c), 0)
    col = lax.broadcasted_iota(jnp.int32, (c, c), 1)
    seqs = range(q_ref.shape[0])
    lg = [jnp.where(valid, _log_sigmoid(jnp.dot(gd_ref[s].astype(bf16), wg_ref[...], preferred_element_type=f32)
                                        + bg_ref[...]) / GATE_TAU, 0.0) for s in seqs]
    n_lvl = e_ref.shape[0] // c - 2

    heads = range(H_C)
    items = [(s, h, u) for s in seqs for h in heads for u in range(nck)]
    sk = {(s, h, u): (s, slice(u * c, (u + 1) * c), slice(h * DK_C, (h + 1) * DK_C)) for s, h, u in items}
    sv = {(s, h, u): (s, slice(u * c, (u + 1) * c), slice(h * DV_C, (h + 1) * DV_C)) for s, h, u in items}
    q = {i: q_ref[sk[i]] * (DK_C ** -0.5) for i in items}
    k = {i: jnp.where(valid[sk[i][1], :], k_ref[sk[i]], 0.0) for i in items}
    vb = {i: v_ref[sv[i]].astype(bf16) for i in items}
    x = {i: _dot01_left(e_ref[...], lg[i[0]][sk[i][1], sk[i][2]], terms=2) for i in items}
    fx = {i: jnp.exp(x[i]) for i in items}
    kv = {i: lax.dot_general((k[i] * fx[i][c:2 * c, :]).astype(bf16), vb[i], _TN, preferred_element_type=f32)
          for i in items}
    a = {i: jnp.where(row == col, jnp.sum(q[i] * k[i], axis=1, keepdims=True), 0.0) for i in items}
    for lvl in range(n_lvl):
        s = 1 << lvl
        right = (rowc // s) % 2 == 1
        same = row // (2 * s) == col // (2 * s)
        fl = {i: fx[i][(2 + lvl) * c:(3 + lvl) * c, :] for i in items}
        qt = {i: jnp.where(right, q[i] * fl[i], 0.0).astype(bf16) for i in items}
        kt = {i: jnp.where(right, 0.0, k[i] * fl[i]).astype(bf16) for i in items}
        p = {i: lax.dot_general(qt[i], kt[i], _NT, preferred_element_type=f32) for i in items}
        a = {i: a[i] + jnp.where(same, p[i], 0.0) for i in items}
    o = {i: jnp.dot(a[i].astype(bf16), vb[i], preferred_element_type=f32) for i in items}
    for s in seqs:
        for h in heads:
            S = S_ref[s, h]
            for u in range(nck):
                i = (s, h, u)
                o[i] = o[i] + jnp.dot((q[i] * fx[i][0:c, :]).astype(bf16), S.astype(bf16),
                                      preferred_element_type=f32)
                b_last = x[i][c - 1:c, :]
                decay = jnp.exp(jnp.transpose(jnp.broadcast_to(b_last, (DK_C, DK_C))))
                S = jnp.concatenate([decay] * (DV_C // DK_C), axis=1) * S + kv[i]
            S_ref[s, h] = S
    for i in items:
        y = o[i] * lax.rsqrt(jnp.mean(o[i] * o[i], axis=1, keepdims=True) + EPS) * gh_ref[:, sv[i][2]]
        y_ref[sv[i]] = (y * jax.nn.silu(r_ref[sv[i]])).astype(y_ref.dtype)


def _gla(q, k, v, r, gd, wg, bg, s0, gh, lo, hi):
    nb, length, _ = q.shape
    c = CHUNK_C
    nck = GLA_CHUNKS_PER_STEP
    ns = GLA_SEQS_PER_STEP if nb % GLA_SEQS_PER_STEP == 0 else 1
    e = _gla_tables(c)
    blk = lambda w: pl.BlockSpec((ns, nck * c, w), lambda b, j: (b, j, 0))
    const = lambda a: pl.BlockSpec(a.shape, lambda b, j: (0,) * a.ndim)
    st = pl.BlockSpec((ns, H_C, DK_C, DV_C), lambda b, j: (b, 0, 0, 0))
    return pl.pallas_call(
        functools.partial(_gla_kernel, c=c, nck=nck, lo=lo, hi=hi),
        grid=(nb // ns, length // (nck * c)),
        in_specs=[blk(H_C * DK_C), blk(H_C * DK_C), blk(H_C * DV_C), blk(H_C * DV_C), blk(LANES),
                  const(wg), const(bg), const(e), st, const(gh)],
        out_specs=[blk(H_C * DV_C), st],
        out_shape=[jax.ShapeDtypeStruct((nb, length, H_C * DV_C), bf16),
                   jax.ShapeDtypeStruct((nb, H_C, DK_C, DV_C), f32)],
        compiler_params=_params("parallel", "arbitrary"),
        name="gla",
    )(q, k, v, r, gd, wg, bg, e, s0, gh)


def _pad_rows(x, rows):
    return jnp.pad(x, ((0, 0), (0, rows - x.shape[1]), (0, 0)))


POST_ROW_TILES = (768, 512, 256, 128)
POST_FF_TILE = 1024


def _row_tile(m, candidates):
    for tm in candidates:
        if m % tm == 0:
            return tm
    raise ValueError(f"row count {m} is not a multiple of {candidates[-1]}")


def kernel(x_prompt, x_sample, cache_sb_k, cache_sb_v, state_mlstm_c, state_mlstm_n, state_mlstm_m, state_gla_s,
           page_table, meta_tokens, norm_gains, w_in_even, b_gate_even, b_sb_even, g_head_even, w_out_even,
           w_in_odd, w_gate_up_odd, b_gate_up_odd, g_head_odd, w_out_odd, w_up, w_down):
    B, seq, D = x_prompt.shape
    DB, T, _ = x_sample.shape
    depth = norm_gains.shape[0]
    n_meta = meta_tokens.shape[0]
    meta_pad = BLOCK - n_meta
    Lp = meta_pad + n_meta + seq
    assert Lp % BLOCK == 0 and (DB * T) % LANES == 0 and T <= SUBLANES

    hp = jnp.concatenate([jnp.zeros((B, meta_pad, D), f32),
                          jnp.broadcast_to(meta_tokens[None], (B, n_meta, D)), x_prompt], axis=1)
    hp = hp.reshape(B * Lp, D)
    hs = x_sample.reshape(DB * T, D)
    tm_p, tm_s = _row_tile(B * Lp, POST_ROW_TILES), _row_tile(DB * T, POST_ROW_TILES)
    tm_in = _row_tile(Lp, (384, 256, 128))

    na, nb_ = H_A * DK_A, H_B * DH_B
    nk, nv = H_C * DK_C, H_C * DV_C
    o_qa, o_ka, o_va, o_oa = 0, na, 2 * na, 3 * na
    o_qb, o_kb, o_vb, o_g = 4 * na, 4 * na + nb_, 4 * na + 2 * nb_, 4 * na + 3 * nb_
    even_common = (_Out(o_qa, na), _Out(o_ka, na), _Out(o_va, na), _Out(o_oa, na), _Out(o_g, LANES),
                   _Out(o_qb, nb_, bf16, DH_B ** -0.5), _Out(o_kb, nb_, bf16), _Out(o_vb, nb_, bf16))
    even_outs_p = even_common + (_Out(o_kb, nb_, transposed=True), _Out(o_vb, nb_, transposed=True))
    even_outs_s = even_common + (_Out(o_kb, nb_), _Out(o_vb, nb_))
    odd_outs = (_Out(0, nk), _Out(nk, nk), _Out(2 * nk, nv), _Out(2 * nk + nv, nv), _Out(2 * nk + 2 * nv, LANES))

    outs = {n: [] for n in ("pk", "pv", "pc", "pn", "pm", "ps", "sk", "sv", "sc", "sn", "sm", "ss")}

    for layer in range(depth):
        gains = norm_gains[layer]
        if layer % 2 == 0:
            e = layer // 2
            w = w_in_even[e]
            w = jnp.concatenate([w[:, :4 * na], w[:, 4 * na + 2 * H_A:], w[:, 4 * na:4 * na + 2 * H_A],
                                 jnp.zeros((D, LANES - 2 * H_A), f32)], axis=1).astype(bf16)
            brow = jnp.pad(b_gate_even[e], (0, LANES - 2 * H_A)).reshape(1, LANES)
            bcol = b_gate_even[e].reshape(2 * H_A, 1)
            gh = g_head_even[e].reshape(1, -1)
            w_out = w_out_even[e].astype(bf16)

            qa, ka, va, oa, gt, qb16, kb16, vb16, kbt, vbt = _in_proj(hp, gains[0:1], w, even_outs_p, tm_in, Lp)
            r3 = lambda a: a.reshape(B, Lp, a.shape[-1])
            g3 = r3(gt)
            ya, C, n, m = _mlstm(r3(qa), r3(ka), r3(va), r3(oa), g3, jnp.swapaxes(g3[:, :, :2 * H_A], 1, 2),
                                 brow, bcol, jnp.zeros((B, H_A, DK_A, DV_A), f32), jnp.zeros((B, H_A, DK_A), f32),
                                 jnp.zeros((B, H_A, 1), f32), gh, meta_pad, Lp)
            hb = _sb_prompt(r3(qb16), r3(kb16), r3(vb16), b_sb_even[e], meta_pad)
            hp = _post([ya.reshape(B * Lp, -1), hb.reshape(B * Lp, -1)], hp, w_out, gains,
                       w_up[layer].astype(bf16), w_down[layer].astype(bf16), tm_p, POST_FF_TILE)
            rows = lambda t: jnp.transpose(t.reshape(B, H_B, DH_B, Lp)[..., meta_pad:], (0, 3, 1, 2))
            outs["pk"].append(rows(kbt))
            outs["pv"].append(rows(vbt))
            outs["pc"].append(C)
            outs["pn"].append(n)
            outs["pm"].append(m.reshape(B, H_A))

            qa, ka, va, oa, gt, qb16, kb16, vb16, kb, vb = _in_proj(hs, gains[0:1], w, even_outs_s, tm_s)
            s3 = lambda a: a.reshape(DB, T, a.shape[-1])
            pc = lambda a: _pad_rows(s3(a), CHUNK_A)
            g3 = pc(gt)
            ya, C, n, m = _mlstm(pc(qa), pc(ka), pc(va), pc(oa), g3, jnp.swapaxes(g3[:, :, :2 * H_A], 1, 2),
                                 brow, bcol, state_mlstm_c[e], state_mlstm_n[e],
                                 state_mlstm_m[e].reshape(DB, H_A, 1), gh, 0, T)
            p8 = lambda a: _pad_rows(s3(a), SUBLANES)
            bias_rows = jnp.broadcast_to(jnp.tile(b_sb_even[e], T)[:, None], (T * H_B, cache_sb_k.shape[2]))
            hb = _sb_sample(p8(qb16.astype(f32)), p8(kb), p8(vb), bias_rows, cache_sb_k, cache_sb_v, page_table, e, T)
            hs = _post([ya[:, :T].reshape(DB * T, -1), hb[:, :T].reshape(DB * T, -1)], hs, w_out, gains,
                       w_up[layer].astype(bf16), w_down[layer].astype(bf16), tm_s, POST_FF_TILE)
            outs["sk"].append(kb.reshape(DB, T, H_B, DH_B))
            outs["sv"].append(vb.reshape(DB, T, H_B, DH_B))
            outs["sc"].append(C)
            outs["sn"].append(n)
            outs["sm"].append(m.reshape(DB, H_A))
        else:
            o = layer // 2
            w = jnp.concatenate([w_in_odd[o], jnp.zeros((D, LANES - GATE_RANK), f32)], axis=1).astype(bf16)
            wg = jnp.concatenate([w_gate_up_odd[o], jnp.zeros((LANES - GATE_RANK, nk), f32)], axis=0).astype(bf16)
            bg = b_gate_up_odd[o].reshape(1, nk)
            gh = g_head_odd[o].reshape(1, -1)
            w_out = w_out_odd[o].astype(bf16)

            q, k, v, r, gd = _in_proj(hp, gains[0:1], w, odd_outs, tm_in)
            r3 = lambda a: a.reshape(B, Lp, a.shape[-1])
            y, S = _gla(r3(q), r3(k), r3(v), r3(r), r3(gd), wg, bg, jnp.zeros((B, H_C, DK_C, DV_C), f32), gh,
                        meta_pad, Lp)
            hp = _post([y.reshape(B * Lp, -1)], hp, w_out, gains,
                       w_up[layer].astype(bf16), w_down[layer].astype(bf16), tm_p, POST_FF_TILE)
            outs["ps"].append(S)

            q, k, v, r, gd = _in_proj(hs, gains[0:1], w, odd_outs, tm_s)
            pc = lambda a: _pad_rows(a.reshape(DB, T, a.shape[-1]), GLA_CHUNKS_PER_STEP * CHUNK_C)
            y, S = _gla(pc(q), pc(k), pc(v), pc(r), pc(gd), wg, bg, state_gla_s[o], gh, 0, T)
            hs = _post([y[:, :T].reshape(DB * T, -1)], hs, w_out, gains,
                       w_up[layer].astype(bf16), w_down[layer].astype(bf16), tm_s, POST_FF_TILE)
            outs["ss"].append(S)

    y_prompt = hp.reshape(B, Lp, D)[:, meta_pad + n_meta:]
    y_sample = hs.reshape(DB, T, D)
    st = jnp.stack
    return (y_prompt, y_sample, st(outs["pk"]), st(outs["pv"]), st(outs["pc"]), st(outs["pn"]), st(outs["pm"]),
            st(outs["ps"]), st(outs["sk"]), st(outs["sv"]), st(outs["sc"]), st(outs["sn"]), st(outs["sm"]),
            st(outs["ss"]))
```

```python
import functools
from typing import Any, NamedTuple

import jax
import jax.numpy as jnp
import numpy as np
from jax import lax
from jax.experimental import pallas as pl
from jax.experimental.pallas import tpu as pltpu

f32 = jnp.float32
bf16 = jnp.bfloat16

BLOCK = 128
H_A, DK_A, DV_A = 4, 128, 128
H_B, DH_B = 8, 64
H_C, DK_C, DV_C = 4, 128, 256
GATE_RANK = 16
GATE_TAU = 16.0
CHUNK_A = 128
CHUNK_C = 64
EPS = 1e-6
LOG2E = 1.4426950408889634

LANES = 128
SUBLANES = 8
VMEM_LIMIT = 48 * 1024 * 1024

_NT = (((1,), (1,)), ((), ()))
_TN = (((0,), (0,)), ((), ()))


def _rms(x, g):
    return x * lax.rsqrt(jnp.mean(x * x, axis=-1, keepdims=True) + EPS) * g


def _softplus(z):
    return jnp.maximum(z, 0.0) + jnp.log(1.0 + jnp.exp2(jnp.abs(z) * (-LOG2E)))


def _log_sigmoid(z):
    return jnp.minimum(z, 0.0) - jnp.log1p(jnp.exp(-jnp.abs(z)))


def _split(x, terms):
    parts = []
    for _ in range(terms - 1):
        p = x.astype(bf16)
        parts.append(p)
        x = x - p.astype(f32)
    parts.append(x.astype(bf16))
    return parts


def _dot01_left(m01, x, terms=3):
    return sum(jnp.dot(m01, p, preferred_element_type=f32) for p in _split(x, terms))


def _dot01_right(x, m01, terms=3):
    return sum(jnp.dot(p, m01, preferred_element_type=f32) for p in _split(x, terms))


def _params(*sem):
    return pltpu.CompilerParams(dimension_semantics=sem, vmem_limit_bytes=VMEM_LIMIT)


class _Out(NamedTuple):
    off: int
    n: int
    dtype: Any = f32
    scale: float = 1.0
    transposed: bool = False


def _in_proj_kernel(x_ref, g_ref, w_ref, *o_refs, outs):
    xn = _rms(x_ref[...], g_ref[...]).astype(bf16)
    cache = {}
    for o_ref, out in zip(o_refs, outs):
        if (out.off, out.n) not in cache:
            cache[(out.off, out.n)] = jnp.dot(xn, w_ref[:, out.off:out.off + out.n], preferred_element_type=f32)
        y = cache[(out.off, out.n)]
        if out.scale != 1.0:
            y = y * out.scale
        if out.transposed:
            o_ref[0] = y.T.astype(o_ref.dtype)
        else:
            o_ref[...] = y.astype(o_ref.dtype)


def _in_proj(x, g, w, outs, tm, seq_len=None):
    m, d = x.shape
    n_tot = w.shape[1]
    seq_len = seq_len or m
    tiles = seq_len // tm
    assert seq_len % tm == 0 and m % seq_len == 0
    return pl.pallas_call(
        functools.partial(_in_proj_kernel, outs=outs),
        grid=(m // tm,),
        in_specs=[pl.BlockSpec((tm, d), lambda i: (i, 0)),
                  pl.BlockSpec((1, d), lambda i: (0, 0)),
                  pl.BlockSpec((d, n_tot), lambda i: (0, 0))],
        out_specs=[pl.BlockSpec((1, o.n, tm), lambda i: (i // tiles, 0, i % tiles)) if o.transposed
                   else pl.BlockSpec((tm, o.n), lambda i: (i, 0)) for o in outs],
        out_shape=[jax.ShapeDtypeStruct((m // seq_len, o.n, seq_len) if o.transposed else (m, o.n), o.dtype)
                   for o in outs],
        compiler_params=_params("parallel"),
        name="in_proj",
    )(x, g, w)


def _post_kernel(*refs, n_a, nf):
    a_refs = refs[:n_a]
    h_ref, wo_ref, g_ref, wu_ref, wd_ref, out_ref, h1_sc, xn_sc, acc_sc = refs[n_a:]
    f = pl.program_id(1)

    @pl.when(f == 0)
    def _():
        if n_a > 1:
            a = jnp.concatenate([r[...] for r in a_refs], axis=-1)
        else:
            a = a_refs[0][...]
        mix = jnp.dot(a, wo_ref[...], preferred_element_type=f32)
        h1 = h_ref[...] + _rms(mix, g_ref[1:2, :])
        h1_sc[...] = h1
        xn_sc[...] = _rms(h1, g_ref[2:3, :]).astype(bf16)
        acc_sc[...] = jnp.zeros_like(acc_sc)

    u = jnp.dot(xn_sc[...], wu_ref[...], preferred_element_type=f32)
    u = jnp.maximum(u, 0.0)
    acc_sc[...] += jnp.dot((u * u).astype(bf16), wd_ref[...], preferred_element_type=f32)

    @pl.when(f == nf - 1)
    def _():
        out_ref[...] = h1_sc[...] + _rms(acc_sc[...], g_ref[3:4, :])


def _post(a_list, h, w_out, gains, w_up, w_down, tm, tf):
    m, d = h.shape
    dff = w_up.shape[1]
    nf = dff // tf
    n_a = len(a_list)
    return pl.pallas_call(
        functools.partial(_post_kernel, n_a=n_a, nf=nf),
        grid=(m // tm, nf),
        in_specs=[pl.BlockSpec((tm, a.shape[1]), lambda i, f: (i, 0)) for a in a_list] + [
            pl.BlockSpec((tm, d), lambda i, f: (i, 0)),
            pl.BlockSpec(w_out.shape, lambda i, f: (0, 0)),
            pl.BlockSpec(gains.shape, lambda i, f: (0, 0)),
            pl.BlockSpec((d, tf), lambda i, f: (0, f)),
            pl.BlockSpec((tf, d), lambda i, f: (f, 0))],
        out_specs=pl.BlockSpec((tm, d), lambda i, f: (i, 0)),
        out_shape=jax.ShapeDtypeStruct((m, d), f32),
        scratch_shapes=[pltpu.VMEM((tm, d), f32), pltpu.VMEM((tm, d), bf16), pltpu.VMEM((tm, d), f32)],
        compiler_params=_params("parallel", "arbitrary"),
        name="post",
    )(*a_list, h, w_out, gains, w_up, w_down)


MLSTM_SEQS_PER_STEP = 4

def _mlstm_kernel(q_ref, k_ref, v_ref, o_ref, g_ref, gt_ref, brow_ref, bcol_ref, c0_ref, n0_ref, m0_ref, gh_ref,
                  y_ref, C_ref, n_ref, m_ref, *, c, lo, hi):
    j = pl.program_id(1)

    @pl.when(j == 0)
    def _():
        C_ref[...] = c0_ref[...]
        n_ref[...] = n0_ref[...]
        m_ref[...] = m0_ref[...]

    row = lax.broadcasted_iota(jnp.int32, (c, c), 0)
    col = lax.broadcasted_iota(jnp.int32, (c, c), 1)
    causal = row >= col
    tril = jnp.where(causal, 1.0, 0.0).astype(bf16)
    triu = jnp.where(row <= col, 1.0, 0.0).astype(bf16)
    pos_c = j * c + lax.broadcasted_iota(jnp.int32, (c, 1), 0)
    valid_c = (pos_c >= lo) & (pos_c < hi)
    pos_r = j * c + lax.broadcasted_iota(jnp.int32, (1, c), 1)
    valid_r = (pos_r >= lo) & (pos_r < hi)

    seqs = range(q_ref.shape[0])
    gc = [g_ref[s] + brow_ref[...] for s in seqs]
    li_c = [jnp.where(valid_c, gc[s], -jnp.inf) for s in seqs]
    lf_c = [jnp.where(valid_c, _log_sigmoid(gc[s]), 0.0) for s in seqs]
    b_c = [_dot01_left(tril, lf_c[s]) for s in seqs]
    gr = [gt_ref[s] + bcol_ref[...] for s in seqs]
    li_r = [jnp.where(valid_r, gr[s], -jnp.inf) for s in seqs]
    lf_r = [jnp.where(valid_r, _log_sigmoid(gr[s]), 0.0) for s in seqs]
    b_r = [_dot01_right(lf_r[s], triu) for s in seqs]

    units = [(s, h) for s in seqs for h in range(H_A)]
    sl = {u: slice(u[1] * DK_A, (u[1] + 1) * DK_A) for u in units}
    q = {u: q_ref[u[0], :, sl[u]] for u in units}
    k = {u: k_ref[u[0], :, sl[u]] * (DK_A ** -0.5) for u in units}
    qb = {u: q[u].astype(bf16) for u in units}
    kb = {u: k[u].astype(bf16) for u in units}
    vb = {u: v_ref[u[0], :, sl[u]].astype(bf16) for u in units}
    bc = {(s, h): b_c[s][:, H_A + h:H_A + h + 1] for s, h in units}
    ic = {(s, h): li_c[s][:, h:h + 1] for s, h in units}
    br = {(s, h): b_r[s][H_A + h:H_A + h + 1, :] for s, h in units}
    ir = {(s, h): li_r[s][h:h + 1, :] for s, h in units}
    m = {(s, h): m_ref[s, h:h + 1, :] for s, h in units}
    C = {(s, h): C_ref[s, h] for s, h in units}
    n = {(s, h): n_ref[s, h:h + 1, :] for s, h in units}

    qk = {u: lax.dot_general(qb[u], kb[u], _NT, preferred_element_type=f32) for u in units}
    qc = {u: jnp.dot(qb[u], C[u].astype(bf16), preferred_element_type=f32) for u in units}
    d_log = {u: jnp.where(causal, bc[u] - br[u] + ir[u], -jnp.inf) for u in units}
    inter = {u: bc[u] + m[u] for u in units}
    m_t = {u: jnp.maximum(inter[u], jnp.max(d_log[u], axis=1, keepdims=True)) for u in units}
    sc = {u: qk[u] * jnp.exp(d_log[u] - m_t[u]) for u in units}
    sv = {u: jnp.dot(sc[u].astype(bf16), vb[u], preferred_element_type=f32) for u in units}
    w_i = {u: jnp.exp(inter[u] - m_t[u]) for u in units}
    den = {u: w_i[u] * jnp.sum(q[u] * n[u], axis=1, keepdims=True) + jnp.sum(sc[u], axis=1, keepdims=True)
           for u in units}

    b_last = {u: bc[u][c - 1:c, :] for u in units}
    gcol = {u: b_last[u] - bc[u] + ic[u] for u in units}
    m_new = {u: jnp.maximum(b_last[u] + m[u], jnp.max(gcol[u], axis=0, keepdims=True)) for u in units}
    w_d = {u: jnp.exp(b_last[u] + m[u] - m_new[u]) for u in units}
    kw = {u: k[u] * jnp.exp(gcol[u] - m_new[u]) for u in units}
    kv = {u: lax.dot_general(kw[u].astype(bf16), vb[u], _TN, preferred_element_type=f32) for u in units}
    for s, h in units:
        u = (s, h)
        C_ref[s, h] = w_d[u] * C[u] + kv[u]
        n_ref[s, h:h + 1, :] = w_d[u] * n[u] + jnp.sum(kw[u], axis=0, keepdims=True)
        m_ref[s, h:h + 1, :] = m_new[u]

    for s, h in units:
        u = (s, h)
        hh = (w_i[u] * qc[u] + sv[u]) / jnp.maximum(jnp.abs(den[u]), jnp.exp(-m_t[u]))
        y = hh * lax.rsqrt(jnp.mean(hh * hh, axis=1, keepdims=True) + EPS) * gh_ref[:, sl[u]]
        y_ref[s, :, sl[u]] = (y * jax.nn.sigmoid(o_ref[s, :, sl[u]])).astype(y_ref.dtype)


def _mlstm(q, k, v, o, g, gt, brow, bcol, c0, n0, m0, gh, lo, hi):
    nb, length, _ = q.shape
    c = CHUNK_A
    ns = MLSTM_SEQS_PER_STEP if nb % MLSTM_SEQS_PER_STEP == 0 else 1
    wide = pl.BlockSpec((ns, c, H_A * DK_A), lambda b, j: (b, j, 0))
    st = lambda shape: pl.BlockSpec((ns,) + shape, lambda b, j: (b,) + (0,) * len(shape))
    const = lambda a: pl.BlockSpec(a.shape, lambda b, j: (0,) * a.ndim)
    return pl.pallas_call(
        functools.partial(_mlstm_kernel, c=c, lo=lo, hi=hi),
        grid=(nb // ns, length // c),
        in_specs=[wide, wide, wide, wide,
                  pl.BlockSpec((ns, c, LANES), lambda b, j: (b, j, 0)),
                  pl.BlockSpec((ns, 2 * H_A, c), lambda b, j: (b, 0, j)),
                  const(brow), const(bcol),
                  st((H_A, DK_A, DV_A)), st((H_A, DK_A)), st((H_A, 1)), const(gh)],
        out_specs=[wide, st((H_A, DK_A, DV_A)), st((H_A, DK_A)), st((H_A, 1))],
        out_shape=[jax.ShapeDtypeStruct((nb, length, H_A * DV_A), bf16),
                   jax.ShapeDtypeStruct((nb, H_A, DK_A, DV_A), f32),
                   jax.ShapeDtypeStruct((nb, H_A, DK_A), f32),
                   jax.ShapeDtypeStruct((nb, H_A, 1), f32)],
        compiler_params=_params("parallel", "arbitrary"),
        name="mlstm",
    )(q, k, v, o, g, gt, brow, bcol, c0, n0, m0, gh)


SB_HEADS = 4


def _neg_cumsum_table(n, inclusive=False):
    j = np.arange(n)[:, None]
    s = np.arange(n)[None, :]
    later = (j >= s) if inclusive else (j > s)
    half = np.concatenate([later.astype(np.float32), np.ones((n, n), np.float32)], axis=1)
    return jnp.asarray(-np.concatenate([half, half], axis=0), bf16)


def _sb_prompt_kernel(bias_ref, q_ref, k_ref, v_ref, tab_ref, o_ref, carry_ref, acc_ref, *, tq, lo):
    hg = pl.program_id(1)
    i = pl.program_id(2)
    sub = tab_ref.shape[0] // 2
    lw = SB_HEADS * DH_B
    lane = lax.broadcasted_iota(jnp.int32, (tq, lw), 1)
    qg = q_ref[0]
    qm = [jnp.where(lane // DH_B == e, qg, jnp.zeros_like(qg)) for e in range(SB_HEADS)]
    bias = [bias_ref[SB_HEADS * hg + e] for e in range(SB_HEADS)]
    heads = range(SB_HEADS)
    carry_ref[...] = jnp.zeros_like(carry_ref)
    acc_ref[...] = jnp.zeros_like(acc_ref)

    def keys(ref, tau):
        return ref[0, pl.ds(pl.multiple_of(tau * tq, tq), tq), :]

    nsub = tq // sub

    def sweep(taus, key_bias):
        nt = len(taus)
        units = [(a, e) for a in range(nt) for e in heads]
        s0 = pl.multiple_of(taus[-1] * tq, tq)
        kt = k_ref[0, pl.ds(s0, nt * tq), :]
        vt = v_ref[0, pl.ds(s0, nt * tq), :]
        zz = [lax.dot_general(qm[e], kt, _NT, preferred_element_type=f32) for e in heads]
        z = {(a, e): zz[e][:, (nt - 1 - a) * tq:(nt - a) * tq] + key_bias[a][e] for a, e in units}
        sp = {u: _softplus(z[u]) for u in units}
        hi = {u: sp[u].astype(bf16) for u in units}
        lo_ = {u: (sp[u] - hi[u].astype(f32)).astype(bf16) for u in units}
        r = {u: [jnp.dot(jnp.concatenate([hi[u][:, j * sub:(j + 1) * sub], lo_[u][:, j * sub:(j + 1) * sub]], axis=1),
                         tab_ref[...], preferred_element_type=f32) for j in range(nsub)] for u in units}
        t = {}
        for e in heads:
            c = carry_ref[e]
            for a in range(len(taus)):
                after = [None] * nsub
                for j in reversed(range(nsub)):
                    after[j] = r[a, e][j][:, :sub] + c
                    c = c + r[a, e][j][:, sub:]
                t[a, e] = z[a, e] + jnp.concatenate(after, axis=1)
            carry_ref[e] = c
        w = {u: jnp.exp(t[u]).astype(bf16) for u in units}
        for e in heads:
            we = jnp.concatenate([w[a, e] for a in reversed(range(nt))], axis=1) if nt > 1 else w[0, e]
            acc_ref[e] += jnp.dot(we, vt, preferred_element_type=f32)

    def row_bias(tau):
        spos = tau * tq + lax.broadcasted_iota(jnp.int32, (1, tq), 1)
        return [jnp.where(spos >= lo, b, -jnp.inf) for b in bias]

    row = lax.broadcasted_iota(jnp.int32, (tq, tq), 0)
    col = lax.broadcasted_iota(jnp.int32, (tq, tq), 1)
    readable = (col < row) & (i * tq + col >= lo)
    sweep([i], [[jnp.where(readable, b, -jnp.inf) for b in bias]])

    @pl.when(i % 2 == 1)
    def _():
        sweep([i - 1], [row_bias(i - 1)])

    def body(n, carry):
        tau = i - 1 - i % 2 - 2 * n
        sweep([tau, tau - 1], [row_bias(tau), row_bias(tau - 1)])
        return carry

    lax.fori_loop(0, i // 2, body, 0)
    out = acc_ref[0]
    for e in range(1, SB_HEADS):
        out = jnp.where(lane // DH_B == e, acc_ref[e], out)
    o_ref[0] = out.astype(o_ref.dtype)


def _sb_prompt(q, k, v, bias, lo):
    nb, length, width = q.shape
    nblk = length // BLOCK
    tq = BLOCK * max(d for d in (1, 2, 3, 4) if nblk % d == 0)
    lw = SB_HEADS * DH_B
    blk = pl.BlockSpec((1, tq, lw), lambda b, hg, i: (b, i, hg))
    full = pl.BlockSpec((1, length, lw), lambda b, hg, i: (b, 0, hg))
    tab = _neg_cumsum_table(BLOCK, inclusive=True)
    return pl.pallas_call(
        functools.partial(_sb_prompt_kernel, tq=tq, lo=lo),
        grid=(nb, width // lw, length // tq),
        in_specs=[pl.BlockSpec(memory_space=pltpu.SMEM), blk, full, full,
                  pl.BlockSpec(tab.shape, lambda b, hg, i: (0, 0))],
        out_specs=blk,
        out_shape=jax.ShapeDtypeStruct((nb, length, width), bf16),
        scratch_shapes=[pltpu.VMEM((SB_HEADS, tq, BLOCK), f32), pltpu.VMEM((SB_HEADS, tq, lw), f32)],
        compiler_params=_params("parallel", "parallel", "arbitrary"),
        name="sb_prompt",
    )(bias, q, k, v, tab)


QH = 32
SB_PAGES_PER_STEP = 32


def _sb_sample_kernel(pt_ref, q_ref, kn_ref, vn_ref, bias_ref, tab_ref, *rest, n_pg, n_grp, n_tok):
    k_refs = rest[:n_pg]
    v_refs = rest[n_pg:2 * n_pg]
    o_ref, carry_sc, acc_sc = rest[2 * n_pg:]
    g = pl.program_id(1)
    width = H_B * DH_B
    psz = tab_ref.shape[0] // 2
    row_head = lax.broadcasted_iota(jnp.int32, (QH, width), 0) % H_B
    lane_head = lax.broadcasted_iota(jnp.int32, (QH, width), 1) // DH_B
    q = q_ref[0]
    qrows = jnp.concatenate([jnp.broadcast_to(q[t:t + 1, :], (H_B, width)) for t in range(n_tok)], axis=0)
    qcat = jnp.where(lane_head == row_head, qrows, jnp.zeros_like(qrows)).astype(bf16)

    def weights(zs):
        n = len(zs)
        z = jnp.concatenate(zs, axis=0) if n > 1 else zs[0]
        sp = _softplus(z)
        hi = sp.astype(bf16)
        lo_ = (sp - hi.astype(f32)).astype(bf16)
        r = jnp.dot(jnp.concatenate([hi, lo_], axis=1), tab_ref[...], preferred_element_type=f32)
        c = carry_sc[...]
        after = [None] * n
        for p in reversed(range(n)):
            after[p] = r[p * QH:(p + 1) * QH, :psz] + c
            c = c + r[p * QH:(p + 1) * QH, psz:]
        carry_sc[...] = c
        w = jnp.exp(z - sp + (jnp.concatenate(after, axis=0) if n > 1 else after[0])).astype(bf16)
        return [w[p * QH:(p + 1) * QH, :] for p in range(n)]

    @pl.when(g == 0)
    def _():
        carry_sc[...] = jnp.zeros_like(carry_sc)
        pad = jnp.zeros((psz - kn_ref.shape[1], width), f32)
        kn = jnp.concatenate([kn_ref[0], pad], axis=0).astype(bf16)
        vn = jnp.concatenate([vn_ref[0], pad], axis=0).astype(bf16)
        j = lax.broadcasted_iota(jnp.int32, (QH, psz), 1)
        t = lax.broadcasted_iota(jnp.int32, (QH, psz), 0) // H_B
        z = lax.dot_general(qcat, kn, _NT, preferred_element_type=f32) + jnp.where(j < t, bias_ref[...], -jnp.inf)
        acc_sc[...] = jnp.dot(weights([z])[0], vn, preferred_element_type=f32)

    ws = weights([jnp.dot(qcat, k_refs[p][0, 0].astype(bf16), preferred_element_type=f32) + bias_ref[...]
                  for p in range(n_pg)])
    acc = acc_sc[...]
    for p in range(n_pg):
        acc = acc + lax.dot_general(ws[p], v_refs[p][0, 0].astype(bf16), _NT, preferred_element_type=f32)
    acc_sc[...] = acc

    @pl.when(g == n_grp - 1)
    def _():
        own = jnp.where(lane_head == row_head, acc_sc[...], 0.0)
        rows = [jnp.sum(own[t * H_B:(t + 1) * H_B, :], axis=0, keepdims=True) for t in range(n_tok)]
        rows.append(jnp.zeros((o_ref.shape[1] - n_tok, width), f32))
        o_ref[0] = jnp.concatenate(rows, axis=0).astype(o_ref.dtype)


def _sb_sample(q, kn, vn, bias, cache_k, cache_v, page_table, layer, n_tok):
    db, rows, width = q.shape
    n_pages = page_table.shape[1]
    n_layers, n_pool, psz = cache_k.shape[:3]
    n_pg = next(d for d in range(min(SB_PAGES_PER_STEP, n_pages), 0, -1) if n_pages % d == 0)
    n_grp = n_pages // n_pg
    assert n_tok * H_B == QH
    tab = _neg_cumsum_table(psz)
    cache_k = jnp.transpose(cache_k, (0, 1, 3, 4, 2)).reshape(n_layers, n_pool, width, psz)
    cache_v = jnp.transpose(cache_v, (0, 1, 3, 4, 2)).reshape(n_layers, n_pool, width, psz)

    def page_spec(p):
        return pl.BlockSpec((1, 1, width, psz),
                            lambda b, g, pt: (layer, pt[b, (n_grp - 1 - g) * n_pg + p], 0, 0))

    small = pl.BlockSpec((1, rows, width), lambda b, g, pt: (b, 0, 0))
    grid_spec = pltpu.PrefetchScalarGridSpec(
        num_scalar_prefetch=1,
        grid=(db, n_grp),
        in_specs=[small, small, small,
                  pl.BlockSpec(bias.shape, lambda b, g, pt: (0, 0)),
                  pl.BlockSpec(tab.shape, lambda b, g, pt: (0, 0))]
        + [page_spec(p) for p in range(n_pg)] * 2,
        out_specs=small,
        scratch_shapes=[pltpu.VMEM((QH, psz), f32), pltpu.VMEM((QH, width), f32)])
    return pl.pallas_call(
        functools.partial(_sb_sample_kernel, n_pg=n_pg, n_grp=n_grp, n_tok=n_tok),
        grid_spec=grid_spec,
        out_shape=jax.ShapeDtypeStruct((db, rows, width), bf16),
        compiler_params=_params("parallel", "arbitrary"),
        name="sb_sample",
    )(page_table, q, kn, vn, bias, tab, *([cache_k] * n_pg), *([cache_v] * n_pg))


GLA_CHUNKS_PER_STEP = 2
GLA_SEQS_PER_STEP = 2


def _gla_tables(c):
    t = np.arange(c)[:, None]
    j = np.arange(c)[None, :]
    mats = [(j <= t), (j > t)]
    s = 1
    while s < c:
        bound = (t // (2 * s)) * (2 * s) + s - 1
        right = (t // s) % 2 == 1
        mats.append(np.where(right, (j > bound) & (j <= t), (j > t) & (j <= bound)))
        s *= 2
    return jnp.asarray(np.concatenate(mats, axis=0).astype(np.float32), bf16)


def _gla_kernel(q_ref, k_ref, v_ref, r_ref, gd_ref, wg_ref, bg_ref, e_ref, s0_ref, gh_ref,
                y_ref, S_ref, *, c, nck, lo, hi):
    j = pl.program_id(1)

    @pl.when(j == 0)
    def _():
        S_ref[...] = s0_ref[...]


    rows = nck * c
    pos = j * rows + lax.broadcasted_iota(jnp.int32, (rows, 1), 0)
    valid = (pos >= lo) & (pos < hi)
    rowc = lax.broadcasted_iota(jnp.int32, (c, 1), 0)
    row = lax.broadcasted_iota(jnp.int32, (c, c), 0)
    col = lax.broadcasted_iota(jnp.int32, (c, c), 1)
    seqs = range(q_ref.shape[0])
    lg = [jnp.where(valid, _log_sigmoid(jnp.dot(gd_ref[s].astype(bf16), wg_ref[...], preferred_element_type=f32)
                                        + bg_ref[...]) / GATE_TAU, 0.0) for s in seqs]
    n_lvl = e_ref.shape[0] // c - 2

    heads = range(H_C)
    items = [(s, h, u) for s in seqs for h in heads for u in range(nck)]
    sk = {(s, h, u): (s, slice(u * c, (u + 1) * c), slice(h * DK_C, (h + 1) * DK_C)) for s, h, u in items}
    sv = {(s, h, u): (s, slice(u * c, (u + 1) * c), slice(h * DV_C, (h + 1) * DV_C)) for s, h, u in items}
    q = {i: q_ref[sk[i]] * (DK_C ** -0.5) for i in items}
    k = {i: jnp.where(valid[sk[i][1], :], k_ref[sk[i]], 0.0) for i in items}
    vb = {i: v_ref[sv[i]].astype(bf16) for i in items}
    x = {i: _dot01_left(e_ref[...], lg[i[0]][sk[i][1], sk[i][2]], terms=2) for i in items}
    fx = {i: jnp.exp(x[i]) for i in items}
    kv = {i: lax.dot_general((k[i] * fx[i][c:2 * c, :]).astype(bf16), vb[i], _TN, preferred_element_type=f32)
          for i in items}
    a = {i: jnp.where(row == col, jnp.sum(q[i] * k[i], axis=1, keepdims=True), 0.0) for i in items}
    for lvl in range(n_lvl):
        s = 1 << lvl
        right = (rowc // s) % 2 == 1
        same = row // (2 * s) == col // (2 * s)
        fl = {i: fx[i][(2 + lvl) * c:(3 + lvl) * c, :] for i in items}
        qt = {i: jnp.where(right, q[i] * fl[i], 0.0).astype(bf16) for i in items}
        kt = {i: jnp.where(right, 0.0, k[i] * fl[i]).astype(bf16) for i in items}
        p = {i: lax.dot_general(qt[i], kt[i], _NT, preferred_element_type=f32) for i in items}
        a = {i: a[i] + jnp.where(same, p[i], 0.0) for i in items}
    o = {i: jnp.dot(a[i].astype(bf16), vb[i], preferred_element_type=f32) for i in items}
    for s in seqs:
        for h in heads:
            S = S_ref[s, h]
            for u in range(nck):
                i = (s, h, u)
                o[i] = o[i] + jnp.dot((q[i] * fx[i][0:c, :]).astype(bf16), S.astype(bf16),
                                      preferred_element_type=f32)
                b_last = x[i][c - 1:c, :]
                decay = jnp.exp(jnp.transpose(jnp.broadcast_to(b_last, (DK_C, DK_C))))
                S = jnp.concatenate([decay] * (DV_C // DK_C), axis=1) * S + kv[i]
            S_ref[s, h] = S
    for i in items:
        y = o[i] * lax.rsqrt(jnp.mean(o[i] * o[i], axis=1, keepdims=True) + EPS) * gh_ref[:, sv[i][2]]
        y_ref[sv[i]] = (y * jax.nn.silu(r_ref[sv[i]])).astype(y_ref.dtype)


def _gla(q, k, v, r, gd, wg, bg, s0, gh, lo, hi):
    nb, length, _ = q.shape
    c = CHUNK_C
    nck = GLA_CHUNKS_PER_STEP
    ns = GLA_SEQS_PER_STEP if nb % GLA_SEQS_PER_STEP == 0 else 1
    e = _gla_tables(c)
    blk = lambda w: pl.BlockSpec((ns, nck * c, w), lambda b, j: (b, j, 0))
    const = lambda a: pl.BlockSpec(a.shape, lambda b, j: (0,) * a.ndim)
    st = pl.BlockSpec((ns, H_C, DK_C, DV_C), lambda b, j: (b, 0, 0, 0))
    return pl.pallas_call(
        functools.partial(_gla_kernel, c=c, nck=nck, lo=lo, hi=hi),
        grid=(nb // ns, length // (nck * c)),
        in_specs=[blk(H_C * DK_C), blk(H_C * DK_C), blk(H_C * DV_C), blk(H_C * DV_C), blk(LANES),
                  const(wg), const(bg), const(e), st, const(gh)],
        out_specs=[blk(H_C * DV_C), st],
        out_shape=[jax.ShapeDtypeStruct((nb, length, H_C * DV_C), bf16),
                   jax.ShapeDtypeStruct((nb, H_C, DK_C, DV_C), f32)],
        compiler_params=_params("parallel", "arbitrary"),
        name="gla",
    )(q, k, v, r, gd, wg, bg, e, s0, gh)


def _pad_rows(x, rows):
    return jnp.pad(x, ((0, 0), (0, rows - x.shape[1]), (0, 0)))


POST_ROW_TILES = (768, 512, 256, 128)
POST_FF_TILE = 2048


def _row_tile(m, candidates):
    for tm in candidates:
        if m % tm == 0:
            return tm
    raise ValueError(f"row count {m} is not a multiple of {candidates[-1]}")


def kernel(x_prompt, x_sample, cache_sb_k, cache_sb_v, state_mlstm_c, state_mlstm_n, state_mlstm_m, state_gla_s,
           page_table, meta_tokens, norm_gains, w_in_even, b_gate_even, b_sb_even, g_head_even, w_out_even,
           w_in_odd, w_gate_up_odd, b_gate_up_odd, g_head_odd, w_out_odd, w_up, w_down):
    B, seq, D = x_prompt.shape
    DB, T, _ = x_sample.shape
    depth = norm_gains.shape[0]
    n_meta = meta_tokens.shape[0]
    meta_pad = BLOCK - n_meta
    Lp = meta_pad + n_meta + seq
    assert Lp % BLOCK == 0 and (DB * T) % LANES == 0 and T <= SUBLANES

    hp = jnp.concatenate([jnp.zeros((B, meta_pad, D), f32),
                          jnp.broadcast_to(meta_tokens[None], (B, n_meta, D)), x_prompt], axis=1)
    hp = hp.reshape(B * Lp, D)
    hs = x_sample.reshape(DB * T, D)
    tm_p, tm_s = _row_tile(B * Lp, POST_ROW_TILES), _row_tile(DB * T, POST_ROW_TILES)
    tm_in = _row_tile(Lp, (384, 256, 128))

    na, nb_ = H_A * DK_A, H_B * DH_B
    nk, nv = H_C * DK_C, H_C * DV_C
    o_qa, o_ka, o_va, o_oa = 0, na, 2 * na, 3 * na
    o_qb, o_kb, o_vb, o_g = 4 * na, 4 * na + nb_, 4 * na + 2 * nb_, 4 * na + 3 * nb_
    even_common = (_Out(o_qa, na), _Out(o_ka, na), _Out(o_va, na), _Out(o_oa, na), _Out(o_g, LANES),
                   _Out(o_qb, nb_, bf16, DH_B ** -0.5), _Out(o_kb, nb_, bf16), _Out(o_vb, nb_, bf16))
    even_outs_p = even_common + (_Out(o_kb, nb_, transposed=True), _Out(o_vb, nb_, transposed=True))
    even_outs_s = even_common + (_Out(o_kb, nb_), _Out(o_vb, nb_))
    odd_outs = (_Out(0, nk), _Out(nk, nk), _Out(2 * nk, nv), _Out(2 * nk + nv, nv), _Out(2 * nk + 2 * nv, LANES))

    outs = {n: [] for n in ("pk", "pv", "pc", "pn", "pm", "ps", "sk", "sv", "sc", "sn", "sm", "ss")}

    for layer in range(depth):
        gains = norm_gains[layer]
        if layer % 2 == 0:
            e = layer // 2
            w = w_in_even[e]
            w = jnp.concatenate([w[:, :4 * na], w[:, 4 * na + 2 * H_A:], w[:, 4 * na:4 * na + 2 * H_A],
                                 jnp.zeros((D, LANES - 2 * H_A), f32)], axis=1).astype(bf16)
            brow = jnp.pad(b_gate_even[e], (0, LANES - 2 * H_A)).reshape(1, LANES)
            bcol = b_gate_even[e].reshape(2 * H_A, 1)
            gh = g_head_even[e].reshape(1, -1)
            w_out = w_out_even[e].astype(bf16)

            qa, ka, va, oa, gt, qb16, kb16, vb16, kbt, vbt = _in_proj(hp, gains[0:1], w, even_outs_p, tm_in, Lp)
            r3 = lambda a: a.reshape(B, Lp, a.shape[-1])
            g3 = r3(gt)
            ya, C, n, m = _mlstm(r3(qa), r3(ka), r3(va), r3(oa), g3, jnp.swapaxes(g3[:, :, :2 * H_A], 1, 2),
                                 brow, bcol, jnp.zeros((B, H_A, DK_A, DV_A), f32), jnp.zeros((B, H_A, DK_A), f32),
                                 jnp.zeros((B, H_A, 1), f32), gh, meta_pad, Lp)
            hb = _sb_prompt(r3(qb16), r3(kb16), r3(vb16), b_sb_even[e], meta_pad)
            hp = _post([ya.reshape(B * Lp, -1), hb.reshape(B * Lp, -1)], hp, w_out, gains,
                       w_up[layer].astype(bf16), w_down[layer].astype(bf16), tm_p, POST_FF_TILE)
            rows = lambda t: jnp.transpose(t.reshape(B, H_B, DH_B, Lp)[..., meta_pad:], (0, 3, 1, 2))
            outs["pk"].append(rows(kbt))
            outs["pv"].append(rows(vbt))
            outs["pc"].append(C)
            outs["pn"].append(n)
            outs["pm"].append(m.reshape(B, H_A))

            qa, ka, va, oa, gt, qb16, kb16, vb16, kb, vb = _in_proj(hs, gains[0:1], w, even_outs_s, tm_s)
            s3 = lambda a: a.reshape(DB, T, a.shape[-1])
            pc = lambda a: _pad_rows(s3(a), CHUNK_A)
            g3 = pc(gt)
            ya, C, n, m = _mlstm(pc(qa), pc(ka), pc(va), pc(oa), g3, jnp.swapaxes(g3[:, :, :2 * H_A], 1, 2),
                                 brow, bcol, state_mlstm_c[e], state_mlstm_n[e],
                                 state_mlstm_m[e].reshape(DB, H_A, 1), gh, 0, T)
            p8 = lambda a: _pad_rows(s3(a), SUBLANES)
            bias_rows = jnp.broadcast_to(jnp.tile(b_sb_even[e], T)[:, None], (T * H_B, cache_sb_k.shape[2]))
            hb = _sb_sample(p8(qb16.astype(f32)), p8(kb), p8(vb), bias_rows, cache_sb_k, cache_sb_v, page_table, e, T)
            hs = _post([ya[:, :T].reshape(DB * T, -1), hb[:, :T].reshape(DB * T, -1)], hs, w_out, gains,
                       w_up[layer].astype(bf16), w_down[layer].astype(bf16), tm_s, POST_FF_TILE)
            outs["sk"].append(kb.reshape(DB, T, H_B, DH_B))
            outs["sv"].append(vb.reshape(DB, T, H_B, DH_B))
            outs["sc"].append(C)
            outs["sn"].append(n)
            outs["sm"].append(m.reshape(DB, H_A))
        else:
            o = layer // 2
            w = jnp.concatenate([w_in_odd[o], jnp.zeros((D, LANES - GATE_RANK), f32)], axis=1).astype(bf16)
            wg = jnp.concatenate([w_gate_up_odd[o], jnp.zeros((LANES - GATE_RANK, nk), f32)], axis=0).astype(bf16)
            bg = b_gate_up_odd[o].reshape(1, nk)
            gh = g_head_odd[o].reshape(1, -1)
            w_out = w_out_odd[o].astype(bf16)

            q, k, v, r, gd = _in_proj(hp, gains[0:1], w, odd_outs, tm_in)
            r3 = lambda a: a.reshape(B, Lp, a.shape[-1])
            y, S = _gla(r3(q), r3(k), r3(v), r3(r), r3(gd), wg, bg, jnp.zeros((B, H_C, DK_C, DV_C), f32), gh,
                        meta_pad, Lp)
            hp = _post([y.reshape(B * Lp, -1)], hp, w_out, gains,
                       w_up[layer].astype(bf16), w_down[layer].astype(bf16), tm_p, POST_FF_TILE)
            outs["ps"].append(S)

            q, k, v, r, gd = _in_proj(hs, gains[0:1], w, odd_outs, tm_s)
            pc = lambda a: _pad_rows(a.reshape(DB, T, a.shape[-1]), GLA_CHUNKS_PER_STEP * CHUNK_C)
            y, S = _gla(pc(q), pc(k), pc(v), pc(r), pc(gd), wg, bg, state_gla_s[o], gh, 0, T)
            hs = _post([y[:, :T].reshape(DB * T, -1)], hs, w_out, gains,
                       w_up[layer].astype(bf16), w_down[layer].astype(bf16), tm_s, POST_FF_TILE)
            outs["ss"].append(S)

    y_prompt = hp.reshape(B, Lp, D)[:, meta_pad + n_meta:]
    y_sample = hs.reshape(DB, T, D)
    st = jnp.stack
    return (y_prompt, y_sample, st(outs["pk"]), st(outs["pv"]), st(outs["pc"]), st(outs["pn"]), st(outs["pm"]),
            st(outs["ps"]), st(outs["sk"]), st(outs["sv"]), st(outs["sc"]), st(outs["sn"]), st(outs["sm"]),
            st(outs["ss"]))
```

```python
import functools
from typing import Any, NamedTuple

import jax
import jax.numpy as jnp
import numpy as np
from jax import lax
from jax.experimental import pallas as pl
from jax.experimental.pallas import tpu as pltpu

f32 = jnp.float32
bf16 = jnp.bfloat16

BLOCK = 128
H_A, DK_A, DV_A = 4, 128, 128
H_B, DH_B = 8, 64
H_C, DK_C, DV_C = 4, 128, 256
GATE_RANK = 16
GATE_TAU = 16.0
CHUNK_A = 128
CHUNK_C = 64
EPS = 1e-6
LOG2E = 1.4426950408889634

LANES = 128
SUBLANES = 8
VMEM_LIMIT = 48 * 1024 * 1024

_NT = (((1,), (1,)), ((), ()))
_TN = (((0,), (0,)), ((), ()))


def _rms(x, g):
    return x * lax.rsqrt(jnp.mean(x * x, axis=-1, keepdims=True) + EPS) * g


def _softplus(z):
    return jnp.maximum(z, 0.0) + jnp.log(1.0 + jnp.exp2(jnp.abs(z) * (-LOG2E)))


def _log_sigmoid(z):
    return jnp.minimum(z, 0.0) - jnp.log1p(jnp.exp(-jnp.abs(z)))


def _split(x, terms):
    parts = []
    for _ in range(terms - 1):
        p = x.astype(bf16)
        parts.append(p)
        x = x - p.astype(f32)
    parts.append(x.astype(bf16))
    return parts


def _dot01_left(m01, x, terms=3):
    return sum(jnp.dot(m01, p, preferred_element_type=f32) for p in _split(x, terms))


def _dot01_right(x, m01, terms=3):
    return sum(jnp.dot(p, m01, preferred_element_type=f32) for p in _split(x, terms))


def _params(*sem):
    return pltpu.CompilerParams(dimension_semantics=sem, vmem_limit_bytes=VMEM_LIMIT)


class _Out(NamedTuple):
    off: int
    n: int
    dtype: Any = f32
    scale: float = 1.0
    transposed: bool = False


def _in_proj_kernel(x_ref, g_ref, w_ref, *o_refs, outs):
    xn = _rms(x_ref[...], g_ref[...]).astype(bf16)
    cache = {}
    for o_ref, out in zip(o_refs, outs):
        if (out.off, out.n) not in cache:
            cache[(out.off, out.n)] = jnp.dot(xn, w_ref[:, out.off:out.off + out.n], preferred_element_type=f32)
        y = cache[(out.off, out.n)]
        if out.scale != 1.0:
            y = y * out.scale
        if out.transposed:
            o_ref[0] = y.T.astype(o_ref.dtype)
        else:
            o_ref[...] = y.astype(o_ref.dtype)


def _in_proj(x, g, w, outs, tm, seq_len=None):
    m, d = x.shape
    n_tot = w.shape[1]
    seq_len = seq_len or m
    tiles = seq_len // tm
    assert seq_len % tm == 0 and m % seq_len == 0
    return pl.pallas_call(
        functools.partial(_in_proj_kernel, outs=outs),
        grid=(m // tm,),
        in_specs=[pl.BlockSpec((tm, d), lambda i: (i, 0)),
                  pl.BlockSpec((1, d), lambda i: (0, 0)),
                  pl.BlockSpec((d, n_tot), lambda i: (0, 0))],
        out_specs=[pl.BlockSpec((1, o.n, tm), lambda i: (i // tiles, 0, i % tiles)) if o.transposed
                   else pl.BlockSpec((tm, o.n), lambda i: (i, 0)) for o in outs],
        out_shape=[jax.ShapeDtypeStruct((m // seq_len, o.n, seq_len) if o.transposed else (m, o.n), o.dtype)
                   for o in outs],
        compiler_params=_params("parallel"),
        name="in_proj",
    )(x, g, w)


def _post_kernel(*refs, n_a, nf):
    a_refs = refs[:n_a]
    h_ref, wo_ref, g_ref, wu_ref, wd_ref, out_ref, h1_sc, xn_sc, acc_sc = refs[n_a:]
    f = pl.program_id(1)

    @pl.when(f == 0)
    def _():
        if n_a > 1:
            a = jnp.concatenate([r[...] for r in a_refs], axis=-1)
        else:
            a = a_refs[0][...]
        mix = jnp.dot(a, wo_ref[...], preferred_element_type=f32)
        h1 = h_ref[...] + _rms(mix, g_ref[1:2, :])
        h1_sc[...] = h1
        xn_sc[...] = _rms(h1, g_ref[2:3, :]).astype(bf16)
        acc_sc[...] = jnp.zeros_like(acc_sc)

    u = jnp.dot(xn_sc[...], wu_ref[...], preferred_element_type=f32)
    u = jnp.maximum(u, 0.0)
    acc_sc[...] += jnp.dot((u * u).astype(bf16), wd_ref[...], preferred_element_type=f32)

    @pl.when(f == nf - 1)
    def _():
        out_ref[...] = h1_sc[...] + _rms(acc_sc[...], g_ref[3:4, :])


def _post(a_list, h, w_out, gains, w_up, w_down, tm, tf):
    m, d = h.shape
    dff = w_up.shape[1]
    nf = dff // tf
    n_a = len(a_list)
    return pl.pallas_call(
        functools.partial(_post_kernel, n_a=n_a, nf=nf),
        grid=(m // tm, nf),
        in_specs=[pl.BlockSpec((tm, a.shape[1]), lambda i, f: (i, 0)) for a in a_list] + [
            pl.BlockSpec((tm, d), lambda i, f: (i, 0)),
            pl.BlockSpec(w_out.shape, lambda i, f: (0, 0)),
            pl.BlockSpec(gains.shape, lambda i, f: (0, 0)),
            pl.BlockSpec((d, tf), lambda i, f: (0, f)),
            pl.BlockSpec((tf, d), lambda i, f: (f, 0))],
        out_specs=pl.BlockSpec((tm, d), lambda i, f: (i, 0)),
        out_shape=jax.ShapeDtypeStruct((m, d), f32),
        scratch_shapes=[pltpu.VMEM((tm, d), f32), pltpu.VMEM((tm, d), bf16), pltpu.VMEM((tm, d), f32)],
        compiler_params=_params("parallel", "arbitrary"),
        name="post",
    )(*a_list, h, w_out, gains, w_up, w_down)


MLSTM_SEQS_PER_STEP = 4

def _mlstm_kernel(q_ref, k_ref, v_ref, o_ref, g_ref, gt_ref, brow_ref, bcol_ref, c0_ref, n0_ref, m0_ref, gh_ref,
                  y_ref, C_ref, n_ref, m_ref, *, c, lo, hi):
    j = pl.program_id(1)

    @pl.when(j == 0)
    def _():
        C_ref[...] = c0_ref[...]
        n_ref[...] = n0_ref[...]
        m_ref[...] = m0_ref[...]

    row = lax.broadcasted_iota(jnp.int32, (c, c), 0)
    col = lax.broadcasted_iota(jnp.int32, (c, c), 1)
    causal = row >= col
    tril = jnp.where(causal, 1.0, 0.0).astype(bf16)
    triu = jnp.where(row <= col, 1.0, 0.0).astype(bf16)
    pos_c = j * c + lax.broadcasted_iota(jnp.int32, (c, 1), 0)
    valid_c = (pos_c >= lo) & (pos_c < hi)
    pos_r = j * c + lax.broadcasted_iota(jnp.int32, (1, c), 1)
    valid_r = (pos_r >= lo) & (pos_r < hi)

    seqs = range(q_ref.shape[0])
    gc = [g_ref[s] + brow_ref[...] for s in seqs]
    li_c = [jnp.where(valid_c, gc[s], -jnp.inf) for s in seqs]
    lf_c = [jnp.where(valid_c, _log_sigmoid(gc[s]), 0.0) for s in seqs]
    b_c = [_dot01_left(tril, lf_c[s]) for s in seqs]
    gr = [gt_ref[s] + bcol_ref[...] for s in seqs]
    li_r = [jnp.where(valid_r, gr[s], -jnp.inf) for s in seqs]
    lf_r = [jnp.where(valid_r, _log_sigmoid(gr[s]), 0.0) for s in seqs]
    b_r = [_dot01_right(lf_r[s], triu) for s in seqs]

    units = [(s, h) for s in seqs for h in range(H_A)]
    sl = {u: slice(u[1] * DK_A, (u[1] + 1) * DK_A) for u in units}
    q = {u: q_ref[u[0], :, sl[u]] for u in units}
    k = {u: k_ref[u[0], :, sl[u]] * (DK_A ** -0.5) for u in units}
    qb = {u: q[u].astype(bf16) for u in units}
    kb = {u: k[u].astype(bf16) for u in units}
    vb = {u: v_ref[u[0], :, sl[u]].astype(bf16) for u in units}
    bc = {(s, h): b_c[s][:, H_A + h:H_A + h + 1] for s, h in units}
    ic = {(s, h): li_c[s][:, h:h + 1] for s, h in units}
    br = {(s, h): b_r[s][H_A + h:H_A + h + 1, :] for s, h in units}
    ir = {(s, h): li_r[s][h:h + 1, :] for s, h in units}
    m = {(s, h): m_ref[s, h:h + 1, :] for s, h in units}
    C = {(s, h): C_ref[s, h] for s, h in units}
    n = {(s, h): n_ref[s, h:h + 1, :] for s, h in units}

    qk = {u: lax.dot_general(qb[u], kb[u], _NT, preferred_element_type=f32) for u in units}
    qc = {u: jnp.dot(qb[u], C[u].astype(bf16), preferred_element_type=f32) for u in units}
    d_log = {u: jnp.where(causal, bc[u] - br[u] + ir[u], -jnp.inf) for u in units}
    inter = {u: bc[u] + m[u] for u in units}
    m_t = {u: jnp.maximum(inter[u], jnp.max(d_log[u], axis=1, keepdims=True)) for u in units}
    sc = {u: qk[u] * jnp.exp(d_log[u] - m_t[u]) for u in units}
    sv = {u: jnp.dot(sc[u].astype(bf16), vb[u], preferred_element_type=f32) for u in units}
    w_i = {u: jnp.exp(inter[u] - m_t[u]) for u in units}
    den = {u: w_i[u] * jnp.sum(q[u] * n[u], axis=1, keepdims=True) + jnp.sum(sc[u], axis=1, keepdims=True)
           for u in units}

    b_last = {u: bc[u][c - 1:c, :] for u in units}
    gcol = {u: b_last[u] - bc[u] + ic[u] for u in units}
    m_new = {u: jnp.maximum(b_last[u] + m[u], jnp.max(gcol[u], axis=0, keepdims=True)) for u in units}
    w_d = {u: jnp.exp(b_last[u] + m[u] - m_new[u]) for u in units}
    kw = {u: k[u] * jnp.exp(gcol[u] - m_new[u]) for u in units}
    kv = {u: lax.dot_general(kw[u].astype(bf16), vb[u], _TN, preferred_element_type=f32) for u in units}
    for s, h in units:
        u = (s, h)
        C_ref[s, h] = w_d[u] * C[u] + kv[u]
        n_ref[s, h:h + 1, :] = w_d[u] * n[u] + jnp.sum(kw[u], axis=0, keepdims=True)
        m_ref[s, h:h + 1, :] = m_new[u]

    for s, h in units:
        u = (s, h)
        hh = (w_i[u] * qc[u] + sv[u]) / jnp.maximum(jnp.abs(den[u]), jnp.exp(-m_t[u]))
        y = hh * lax.rsqrt(jnp.mean(hh * hh, axis=1, keepdims=True) + EPS) * gh_ref[:, sl[u]]
        y_ref[s, :, sl[u]] = (y * jax.nn.sigmoid(o_ref[s, :, sl[u]])).astype(y_ref.dtype)


def _mlstm(q, k, v, o, g, gt, brow, bcol, c0, n0, m0, gh, lo, hi):
    nb, length, _ = q.shape
    c = CHUNK_A
    ns = MLSTM_SEQS_PER_STEP if nb % MLSTM_SEQS_PER_STEP == 0 else 1
    wide = pl.BlockSpec((ns, c, H_A * DK_A), lambda b, j: (b, j, 0))
    st = lambda shape: pl.BlockSpec((ns,) + shape, lambda b, j: (b,) + (0,) * len(shape))
    const = lambda a: pl.BlockSpec(a.shape, lambda b, j: (0,) * a.ndim)
    return pl.pallas_call(
        functools.partial(_mlstm_kernel, c=c, lo=lo, hi=hi),
        grid=(nb // ns, length // c),
        in_specs=[wide, wide, wide, wide,
                  pl.BlockSpec((ns, c, LANES), lambda b, j: (b, j, 0)),
                  pl.BlockSpec((ns, 2 * H_A, c), lambda b, j: (b, 0, j)),
                  const(brow), const(bcol),
                  st((H_A, DK_A, DV_A)), st((H_A, DK_A)), st((H_A, 1)), const(gh)],
        out_specs=[wide, st((H_A, DK_A, DV_A)), st((H_A, DK_A)), st((H_A, 1))],
        out_shape=[jax.ShapeDtypeStruct((nb, length, H_A * DV_A), bf16),
                   jax.ShapeDtypeStruct((nb, H_A, DK_A, DV_A), f32),
                   jax.ShapeDtypeStruct((nb, H_A, DK_A), f32),
                   jax.ShapeDtypeStruct((nb, H_A, 1), f32)],
        compiler_params=_params("parallel", "arbitrary"),
        name="mlstm",
    )(q, k, v, o, g, gt, brow, bcol, c0, n0, m0, gh)


SB_HEADS = 4


def _neg_cumsum_table(n, inclusive=False):
    j = np.arange(n)[:, None]
    s = np.arange(n)[None, :]
    later = (j >= s) if inclusive else (j > s)
    half = np.concatenate([later.astype(np.float32), np.ones((n, n), np.float32)], axis=1)
    return jnp.asarray(-np.concatenate([half, half], axis=0), bf16)


def _sb_prompt_kernel(bias_ref, q_ref, k_ref, v_ref, tab_ref, o_ref, carry_ref, acc_ref, *, tq, lo):
    hg = pl.program_id(1)
    i = pl.program_id(2)
    sub = tab_ref.shape[0] // 2
    lw = SB_HEADS * DH_B
    lane = lax.broadcasted_iota(jnp.int32, (tq, lw), 1)
    qg = q_ref[0]
    qm = [jnp.where(lane // DH_B == e, qg, jnp.zeros_like(qg)) for e in range(SB_HEADS)]
    bias = [bias_ref[SB_HEADS * hg + e] for e in range(SB_HEADS)]
    heads = range(SB_HEADS)
    carry_ref[...] = jnp.zeros_like(carry_ref)
    acc_ref[...] = jnp.zeros_like(acc_ref)

    def keys(ref, tau):
        return ref[0, pl.ds(pl.multiple_of(tau * tq, tq), tq), :]

    nsub = tq // sub

    def sweep(taus, key_bias):
        nt = len(taus)
        units = [(a, e) for a in range(nt) for e in heads]
        s0 = pl.multiple_of(taus[-1] * tq, tq)
        kt = k_ref[0, pl.ds(s0, nt * tq), :]
        vt = v_ref[0, pl.ds(s0, nt * tq), :]
        zz = [lax.dot_general(qm[e], kt, _NT, preferred_element_type=f32) for e in heads]
        z = {(a, e): zz[e][:, (nt - 1 - a) * tq:(nt - a) * tq] + key_bias[a][e] for a, e in units}
        sp = {u: _softplus(z[u]) for u in units}
        hi = {u: sp[u].astype(bf16) for u in units}
        lo_ = {u: (sp[u] - hi[u].astype(f32)).astype(bf16) for u in units}
        r = {u: [jnp.dot(jnp.concatenate([hi[u][:, j * sub:(j + 1) * sub], lo_[u][:, j * sub:(j + 1) * sub]], axis=1),
                         tab_ref[...], preferred_element_type=f32) for j in range(nsub)] for u in units}
        t = {}
        for e in heads:
            c = carry_ref[e]
            for a in range(len(taus)):
                after = [None] * nsub
                for j in reversed(range(nsub)):
                    after[j] = r[a, e][j][:, :sub] + c
                    c = c + r[a, e][j][:, sub:]
                t[a, e] = z[a, e] + jnp.concatenate(after, axis=1)
            carry_ref[e] = c
        w = {u: jnp.exp(t[u]).astype(bf16) for u in units}
        for e in heads:
            we = jnp.concatenate([w[a, e] for a in reversed(range(nt))], axis=1) if nt > 1 else w[0, e]
            acc_ref[e] += jnp.dot(we, vt, preferred_element_type=f32)

    def row_bias(tau):
        spos = tau * tq + lax.broadcasted_iota(jnp.int32, (1, tq), 1)
        return [jnp.where(spos >= lo, b, -jnp.inf) for b in bias]

    row = lax.broadcasted_iota(jnp.int32, (tq, tq), 0)
    col = lax.broadcasted_iota(jnp.int32, (tq, tq), 1)
    readable = (col < row) & (i * tq + col >= lo)
    sweep([i], [[jnp.where(readable, b, -jnp.inf) for b in bias]])

    @pl.when(i % 2 == 1)
    def _():
        sweep([i - 1], [row_bias(i - 1)])

    def body(n, carry):
        tau = i - 1 - i % 2 - 2 * n
        sweep([tau, tau - 1], [row_bias(tau), row_bias(tau - 1)])
        return carry

    lax.fori_loop(0, i // 2, body, 0)
    out = acc_ref[0]
    for e in range(1, SB_HEADS):
        out = jnp.where(lane // DH_B == e, acc_ref[e], out)
    o_ref[0] = out.astype(o_ref.dtype)


def _sb_prompt(q, k, v, bias, lo):
    nb, length, width = q.shape
    nblk = length // BLOCK
    tq = BLOCK * max(d for d in (1, 2, 3, 4) if nblk % d == 0)
    lw = SB_HEADS * DH_B
    blk = pl.BlockSpec((1, tq, lw), lambda b, hg, i: (b, i, hg))
    full = pl.BlockSpec((1, length, lw), lambda b, hg, i: (b, 0, hg))
    tab = _neg_cumsum_table(BLOCK, inclusive=True)
    return pl.pallas_call(
        functools.partial(_sb_prompt_kernel, tq=tq, lo=lo),
        grid=(nb, width // lw, length // tq),
        in_specs=[pl.BlockSpec(memory_space=pltpu.SMEM), blk, full, full,
                  pl.BlockSpec(tab.shape, lambda b, hg, i: (0, 0))],
        out_specs=blk,
        out_shape=jax.ShapeDtypeStruct((nb, length, width), bf16),
        scratch_shapes=[pltpu.VMEM((SB_HEADS, tq, BLOCK), f32), pltpu.VMEM((SB_HEADS, tq, lw), f32)],
        compiler_params=_params("parallel", "parallel", "arbitrary"),
        name="sb_prompt",
    )(bias, q, k, v, tab)


QH = 32
SB_PAGES_PER_STEP = 32


def _sb_sample_kernel(pt_ref, q_ref, kn_ref, vn_ref, bias_ref, tab_ref, *rest, n_pg, n_grp, n_tok):
    k_refs = rest[:n_pg]
    v_refs = rest[n_pg:2 * n_pg]
    o_ref, carry_sc, acc_sc = rest[2 * n_pg:]
    g = pl.program_id(1)
    width = H_B * DH_B
    psz = tab_ref.shape[0] // 2
    row_head = lax.broadcasted_iota(jnp.int32, (QH, width), 0) % H_B
    lane_head = lax.broadcasted_iota(jnp.int32, (QH, width), 1) // DH_B
    q = q_ref[0]
    qrows = jnp.concatenate([jnp.broadcast_to(q[t:t + 1, :], (H_B, width)) for t in range(n_tok)], axis=0)
    qcat = jnp.where(lane_head == row_head, qrows, jnp.zeros_like(qrows)).astype(bf16)

    def weights(zs):
        n = len(zs)
        z = jnp.concatenate(zs, axis=0) if n > 1 else zs[0]
        sp = _softplus(z)
        hi = sp.astype(bf16)
        lo_ = (sp - hi.astype(f32)).astype(bf16)
        r = jnp.dot(jnp.concatenate([hi, lo_], axis=1), tab_ref[...], preferred_element_type=f32)
        c = carry_sc[...]
        after = [None] * n
        for p in reversed(range(n)):
            after[p] = r[p * QH:(p + 1) * QH, :psz] + c
            c = c + r[p * QH:(p + 1) * QH, psz:]
        carry_sc[...] = c
        w = jnp.exp(z - sp + (jnp.concatenate(after, axis=0) if n > 1 else after[0])).astype(bf16)
        return [w[p * QH:(p + 1) * QH, :] for p in range(n)]

    @pl.when(g == 0)
    def _():
        carry_sc[...] = jnp.zeros_like(carry_sc)
        pad = jnp.zeros((psz - kn_ref.shape[1], width), f32)
        kn = jnp.concatenate([kn_ref[0], pad], axis=0).astype(bf16)
        vn = jnp.concatenate([vn_ref[0], pad], axis=0).astype(bf16)
        j = lax.broadcasted_iota(jnp.int32, (QH, psz), 1)
        t = lax.broadcasted_iota(jnp.int32, (QH, psz), 0) // H_B
        z = lax.dot_general(qcat, kn, _NT, preferred_element_type=f32) + jnp.where(j < t, bias_ref[...], -jnp.inf)
        acc_sc[...] = jnp.dot(weights([z])[0], vn, preferred_element_type=f32)

    ws = weights([jnp.dot(qcat, k_refs[p][0, 0].astype(bf16), preferred_element_type=f32) + bias_ref[...]
                  for p in range(n_pg)])
    acc = acc_sc[...]
    for p in range(n_pg):
        acc = acc + lax.dot_general(ws[p], v_refs[p][0, 0].astype(bf16), _NT, preferred_element_type=f32)
    acc_sc[...] = acc

    @pl.when(g == n_grp - 1)
    def _():
        own = jnp.where(lane_head == row_head, acc_sc[...], 0.0)
        rows = [jnp.sum(own[t * H_B:(t + 1) * H_B, :], axis=0, keepdims=True) for t in range(n_tok)]
        rows.append(jnp.zeros((o_ref.shape[1] - n_tok, width), f32))
        o_ref[0] = jnp.concatenate(rows, axis=0).astype(o_ref.dtype)


def _sb_sample(q, kn, vn, bias, cache_k, cache_v, page_table, layer, n_tok):
    db, rows, width = q.shape
    n_pages = page_table.shape[1]
    n_layers, n_pool, psz = cache_k.shape[:3]
    n_pg = next(d for d in range(min(SB_PAGES_PER_STEP, n_pages), 0, -1) if n_pages % d == 0)
    n_grp = n_pages // n_pg
    assert n_tok * H_B == QH
    tab = _neg_cumsum_table(psz)
    cache_k = jnp.transpose(cache_k, (0, 1, 3, 4, 2)).reshape(n_layers, n_pool, width, psz)
    cache_v = jnp.transpose(cache_v, (0, 1, 3, 4, 2)).reshape(n_layers, n_pool, width, psz)

    def page_spec(p):
        return pl.BlockSpec((1, 1, width, psz),
                            lambda b, g, pt: (layer, pt[b, (n_grp - 1 - g) * n_pg + p], 0, 0))

    small = pl.BlockSpec((1, rows, width), lambda b, g, pt: (b, 0, 0))
    grid_spec = pltpu.PrefetchScalarGridSpec(
        num_scalar_prefetch=1,
        grid=(db, n_grp),
        in_specs=[small, small, small,
                  pl.BlockSpec(bias.shape, lambda b, g, pt: (0, 0)),
                  pl.BlockSpec(tab.shape, lambda b, g, pt: (0, 0))]
        + [page_spec(p) for p in range(n_pg)] * 2,
        out_specs=small,
        scratch_shapes=[pltpu.VMEM((QH, psz), f32), pltpu.VMEM((QH, width), f32)])
    return pl.pallas_call(
        functools.partial(_sb_sample_kernel, n_pg=n_pg, n_grp=n_grp, n_tok=n_tok),
        grid_spec=grid_spec,
        out_shape=jax.ShapeDtypeStruct((db, rows, width), bf16),
        compiler_params=_params("parallel", "arbitrary"),
        name="sb_sample",
    )(page_table, q, kn, vn, bias, tab, *([cache_k] * n_pg), *([cache_v] * n_pg))


GLA_CHUNKS_PER_STEP = 2
GLA_SEQS_PER_STEP = 2


def _gla_tables(c):
    t = np.arange(c)[:, None]
    j = np.arange(c)[None, :]
    mats = [(j <= t), (j > t)]
    s = 1
    while s < c:
        bound = (t // (2 * s)) * (2 * s) + s - 1
        right = (t // s) % 2 == 1
        mats.append(np.where(right, (j > bound) & (j <= t), (j > t) & (j <= bound)))
        s *= 2
    return jnp.asarray(np.concatenate(mats, axis=0).astype(np.float32), bf16)


def _gla_kernel(q_ref, k_ref, v_ref, r_ref, gd_ref, wg_ref, bg_ref, e_ref, s0_ref, gh_ref,
                y_ref, S_ref, *, c, nck, lo, hi):
    j = pl.program_id(1)

    @pl.when(j == 0)
    def _():
        S_ref[...] = s0_ref[...]


    rows = nck * c
    pos = j * rows + lax.broadcasted_iota(jnp.int32, (rows, 1), 0)
    valid = (pos >= lo) & (pos < hi)
    rowc = lax.broadcasted_iota(jnp.int32, (c, 1), 0)
    row = lax.broadcasted_iota(jnp.int32, (c, c), 0)
    col = lax.broadcasted_iota(jnp.int32, (c, c), 1)
    seqs = range(q_ref.shape[0])
    lg = [jnp.where(valid, _log_sigmoid(jnp.dot(gd_ref[s].astype(bf16), wg_ref[...], preferred_element_type=f32)
                                        + bg_ref[...]) / GATE_TAU, 0.0) for s in seqs]
    n_lvl = e_ref.shape[0] // c - 2

    heads = range(H_C)
    items = [(s, h, u) for s in seqs for h in heads for u in range(nck)]
    sk = {(s, h, u): (s, slice(u * c, (u + 1) * c), slice(h * DK_C, (h + 1) * DK_C)) for s, h, u in items}
    sv = {(s, h, u): (s, slice(u * c, (u + 1) * c), slice(h * DV_C, (h + 1) * DV_C)) for s, h, u in items}
    q = {i: q_ref[sk[i]] * (DK_C ** -0.5) for i in items}
    k = {i: jnp.where(valid[sk[i][1], :], k_ref[sk[i]], 0.0) for i in items}
    vb = {i: v_ref[sv[i]].astype(bf16) for i in items}
    x = {i: _dot01_left(e_ref[...], lg[i[0]][sk[i][1], sk[i][2]], terms=2) for i in items}
    fx = {i: jnp.exp(x[i]) for i in items}
    kv = {i: lax.dot_general((k[i] * fx[i][c:2 * c, :]).astype(bf16), vb[i], _TN, preferred_element_type=f32)
          for i in items}
    a = {i: jnp.where(row == col, jnp.sum(q[i] * k[i], axis=1, keepdims=True), 0.0) for i in items}
    for lvl in range(n_lvl):
        s = 1 << lvl
        right = (rowc // s) % 2 == 1
        same = row // (2 * s) == col // (2 * s)
        fl = {i: fx[i][(2 + lvl) * c:(3 + lvl) * c, :] for i in items}
        qt = {i: jnp.where(right, q[i] * fl[i], 0.0).astype(bf16) for i in items}
        kt = {i: jnp.where(right, 0.0, k[i] * fl[i]).astype(bf16) for i in items}
        p = {i: lax.dot_general(qt[i], kt[i], _NT, preferred_element_type=f32) for i in items}
        a = {i: a[i] + jnp.where(same, p[i], 0.0) for i in items}
    o = {i: jnp.dot(a[i].astype(bf16), vb[i], preferred_element_type=f32) for i in items}
    for s in seqs:
        for h in heads:
            S = S_ref[s, h]
            for u in range(nck):
                i = (s, h, u)
                o[i] = o[i] + jnp.dot((q[i] * fx[i][0:c, :]).astype(bf16), S.astype(bf16),
                                      preferred_element_type=f32)
                b_last = x[i][c - 1:c, :]
                decay = jnp.exp(jnp.transpose(jnp.broadcast_to(b_last, (DK_C, DK_C))))
                S = jnp.concatenate([decay] * (DV_C // DK_C), axis=1) * S + kv[i]
            S_ref[s, h] = S
    for i in items:
        y = o[i] * lax.rsqrt(jnp.mean(o[i] * o[i], axis=1, keepdims=True) + EPS) * gh_ref[:, sv[i][2]]
        y_ref[sv[i]] = (y * jax.nn.silu(r_ref[sv[i]])).astype(y_ref.dtype)


def _gla(q, k, v, r, gd, wg, bg, s0, gh, lo, hi):
    nb, length, _ = q.shape
    c = CHUNK_C
    nck = min(GLA_CHUNKS_PER_STEP, length // c)
    ns = GLA_SEQS_PER_STEP * GLA_CHUNKS_PER_STEP // nck
    ns = ns if nb % ns == 0 else 1
    e = _gla_tables(c)
    blk = lambda w: pl.BlockSpec((ns, nck * c, w), lambda b, j: (b, j, 0))
    const = lambda a: pl.BlockSpec(a.shape, lambda b, j: (0,) * a.ndim)
    st = pl.BlockSpec((ns, H_C, DK_C, DV_C), lambda b, j: (b, 0, 0, 0))
    return pl.pallas_call(
        functools.partial(_gla_kernel, c=c, nck=nck, lo=lo, hi=hi),
        grid=(nb // ns, length // (nck * c)),
        in_specs=[blk(H_C * DK_C), blk(H_C * DK_C), blk(H_C * DV_C), blk(H_C * DV_C), blk(LANES),
                  const(wg), const(bg), const(e), st, const(gh)],
        out_specs=[blk(H_C * DV_C), st],
        out_shape=[jax.ShapeDtypeStruct((nb, length, H_C * DV_C), bf16),
                   jax.ShapeDtypeStruct((nb, H_C, DK_C, DV_C), f32)],
        compiler_params=_params("parallel", "arbitrary"),
        name="gla",
    )(q, k, v, r, gd, wg, bg, e, s0, gh)


def _pad_rows(x, rows):
    return jnp.pad(x, ((0, 0), (0, rows - x.shape[1]), (0, 0)))


POST_ROW_TILES = (768, 512, 256, 128)
POST_FF_TILE = 2048


def _row_tile(m, candidates):
    for tm in candidates:
        if m % tm == 0:
            return tm
    raise ValueError(f"row count {m} is not a multiple of {candidates[-1]}")


def kernel(x_prompt, x_sample, cache_sb_k, cache_sb_v, state_mlstm_c, state_mlstm_n, state_mlstm_m, state_gla_s,
           page_table, meta_tokens, norm_gains, w_in_even, b_gate_even, b_sb_even, g_head_even, w_out_even,
           w_in_odd, w_gate_up_odd, b_gate_up_odd, g_head_odd, w_out_odd, w_up, w_down):
    B, seq, D = x_prompt.shape
    DB, T, _ = x_sample.shape
    depth = norm_gains.shape[0]
    n_meta = meta_tokens.shape[0]
    meta_pad = BLOCK - n_meta
    Lp = meta_pad + n_meta + seq
    assert Lp % BLOCK == 0 and (DB * T) % LANES == 0 and T <= SUBLANES

    hp = jnp.concatenate([jnp.zeros((B, meta_pad, D), f32),
                          jnp.broadcast_to(meta_tokens[None], (B, n_meta, D)), x_prompt], axis=1)
    hp = hp.reshape(B * Lp, D)
    hs = x_sample.reshape(DB * T, D)
    tm_p, tm_s = _row_tile(B * Lp, POST_ROW_TILES), _row_tile(DB * T, POST_ROW_TILES)
    tm_in = _row_tile(Lp, (384, 256, 128))

    na, nb_ = H_A * DK_A, H_B * DH_B
    nk, nv = H_C * DK_C, H_C * DV_C
    o_qa, o_ka, o_va, o_oa = 0, na, 2 * na, 3 * na
    o_qb, o_kb, o_vb, o_g = 4 * na, 4 * na + nb_, 4 * na + 2 * nb_, 4 * na + 3 * nb_
    even_common = (_Out(o_qa, na), _Out(o_ka, na), _Out(o_va, na), _Out(o_oa, na), _Out(o_g, LANES),
                   _Out(o_qb, nb_, bf16, DH_B ** -0.5), _Out(o_kb, nb_, bf16), _Out(o_vb, nb_, bf16))
    even_outs_p = even_common + (_Out(o_kb, nb_, transposed=True), _Out(o_vb, nb_, transposed=True))
    even_outs_s = even_common + (_Out(o_kb, nb_), _Out(o_vb, nb_))
    odd_outs = (_Out(0, nk), _Out(nk, nk), _Out(2 * nk, nv), _Out(2 * nk + nv, nv), _Out(2 * nk + 2 * nv, LANES))

    outs = {n: [] for n in ("pk", "pv", "pc", "pn", "pm", "ps", "sk", "sv", "sc", "sn", "sm", "ss")}

    for layer in range(depth):
        gains = norm_gains[layer]
        if layer % 2 == 0:
            e = layer // 2
            w = w_in_even[e]
            w = jnp.concatenate([w[:, :4 * na], w[:, 4 * na + 2 * H_A:], w[:, 4 * na:4 * na + 2 * H_A],
                                 jnp.zeros((D, LANES - 2 * H_A), f32)], axis=1).astype(bf16)
            brow = jnp.pad(b_gate_even[e], (0, LANES - 2 * H_A)).reshape(1, LANES)
            bcol = b_gate_even[e].reshape(2 * H_A, 1)
            gh = g_head_even[e].reshape(1, -1)
            w_out = w_out_even[e].astype(bf16)

            qa, ka, va, oa, gt, qb16, kb16, vb16, kbt, vbt = _in_proj(hp, gains[0:1], w, even_outs_p, tm_in, Lp)
            r3 = lambda a: a.reshape(B, Lp, a.shape[-1])
            g3 = r3(gt)
            ya, C, n, m = _mlstm(r3(qa), r3(ka), r3(va), r3(oa), g3, jnp.swapaxes(g3[:, :, :2 * H_A], 1, 2),
                                 brow, bcol, jnp.zeros((B, H_A, DK_A, DV_A), f32), jnp.zeros((B, H_A, DK_A), f32),
                                 jnp.zeros((B, H_A, 1), f32), gh, meta_pad, Lp)
            hb = _sb_prompt(r3(qb16), r3(kb16), r3(vb16), b_sb_even[e], meta_pad)
            hp = _post([ya.reshape(B * Lp, -1), hb.reshape(B * Lp, -1)], hp, w_out, gains,
                       w_up[layer].astype(bf16), w_down[layer].astype(bf16), tm_p, POST_FF_TILE)
            rows = lambda t: jnp.transpose(t.reshape(B, H_B, DH_B, Lp)[..., meta_pad:], (0, 3, 1, 2))
            outs["pk"].append(rows(kbt))
            outs["pv"].append(rows(vbt))
            outs["pc"].append(C)
            outs["pn"].append(n)
            outs["pm"].append(m.reshape(B, H_A))

            qa, ka, va, oa, gt, qb16, kb16, vb16, kb, vb = _in_proj(hs, gains[0:1], w, even_outs_s, tm_s)
            s3 = lambda a: a.reshape(DB, T, a.shape[-1])
            pc = lambda a: _pad_rows(s3(a), CHUNK_A)
            g3 = pc(gt)
            ya, C, n, m = _mlstm(pc(qa), pc(ka), pc(va), pc(oa), g3, jnp.swapaxes(g3[:, :, :2 * H_A], 1, 2),
                                 brow, bcol, state_mlstm_c[e], state_mlstm_n[e],
                                 state_mlstm_m[e].reshape(DB, H_A, 1), gh, 0, T)
            p8 = lambda a: _pad_rows(s3(a), SUBLANES)
            bias_rows = jnp.broadcast_to(jnp.tile(b_sb_even[e], T)[:, None], (T * H_B, cache_sb_k.shape[2]))
            hb = _sb_sample(p8(qb16.astype(f32)), p8(kb), p8(vb), bias_rows, cache_sb_k, cache_sb_v, page_table, e, T)
            hs = _post([ya[:, :T].reshape(DB * T, -1), hb[:, :T].reshape(DB * T, -1)], hs, w_out, gains,
                       w_up[layer].astype(bf16), w_down[layer].astype(bf16), tm_s, POST_FF_TILE)
            outs["sk"].append(kb.reshape(DB, T, H_B, DH_B))
            outs["sv"].append(vb.reshape(DB, T, H_B, DH_B))
            outs["sc"].append(C)
            outs["sn"].append(n)
            outs["sm"].append(m.reshape(DB, H_A))
        else:
            o = layer // 2
            w = jnp.concatenate([w_in_odd[o], jnp.zeros((D, LANES - GATE_RANK), f32)], axis=1).astype(bf16)
            wg = jnp.concatenate([w_gate_up_odd[o], jnp.zeros((LANES - GATE_RANK, nk), f32)], axis=0).astype(bf16)
            bg = b_gate_up_odd[o].reshape(1, nk)
            gh = g_head_odd[o].reshape(1, -1)
            w_out = w_out_odd[o].astype(bf16)

            q, k, v, r, gd = _in_proj(hp, gains[0:1], w, odd_outs, tm_in)
            r3 = lambda a: a.reshape(B, Lp, a.shape[-1])
            y, S = _gla(r3(q), r3(k), r3(v), r3(r), r3(gd), wg, bg, jnp.zeros((B, H_C, DK_C, DV_C), f32), gh,
                        meta_pad, Lp)
            hp = _post([y.reshape(B * Lp, -1)], hp, w_out, gains,
                       w_up[layer].astype(bf16), w_down[layer].astype(bf16), tm_p, POST_FF_TILE)
            outs["ps"].append(S)

            q, k, v, r, gd = _in_proj(hs, gains[0:1], w, odd_outs, tm_s)
            pc = lambda a: _pad_rows(a.reshape(DB, T, a.shape[-1]), CHUNK_C)
            y, S = _gla(pc(q), pc(k), pc(v), pc(r), pc(gd), wg, bg, state_gla_s[o], gh, 0, T)
            hs = _post([y[:, :T].reshape(DB * T, -1)], hs, w_out, gains,
                       w_up[layer].astype(bf16), w_down[layer].astype(bf16), tm_s, POST_FF_TILE)
            outs["ss"].append(S)

    y_prompt = hp.reshape(B, Lp, D)[:, meta_pad + n_meta:]
    y_sample = hs.reshape(DB, T, D)
    st = jnp.stack
    return (y_prompt, y_sample, st(outs["pk"]), st(outs["pv"]), st(outs["pc"]), st(outs["pn"]), st(outs["pm"]),
            st(outs["ps"]), st(outs["sk"]), st(outs["sv"]), st(outs["sc"]), st(outs["sn"]), st(outs["sm"]),
            st(outs["ss"]))
```

```python
import functools
from typing import Any, NamedTuple

import jax
import jax.numpy as jnp
import numpy as np
from jax import lax
from jax.experimental import pallas as pl
from jax.experimental.pallas import tpu as pltpu

f32 = jnp.float32
bf16 = jnp.bfloat16

BLOCK = 128
H_A, DK_A, DV_A = 4, 128, 128
H_B, DH_B = 8, 64
H_C, DK_C, DV_C = 4, 128, 256
GATE_RANK = 16
GATE_TAU = 16.0
CHUNK_A = 128
CHUNK_C = 64
EPS = 1e-6
LOG2E = 1.4426950408889634

LANES = 128
SUBLANES = 8
VMEM_LIMIT = 48 * 1024 * 1024

_NT = (((1,), (1,)), ((), ()))
_TN = (((0,), (0,)), ((), ()))


def _rms(x, g):
    return x * lax.rsqrt(jnp.mean(x * x, axis=-1, keepdims=True) + EPS) * g


def _softplus(z):
    return jnp.maximum(z, 0.0) + jnp.log(1.0 + jnp.exp2(jnp.abs(z) * (-LOG2E)))


def _log_sigmoid(z):
    return jnp.minimum(z, 0.0) - jnp.log1p(jnp.exp(-jnp.abs(z)))


def _split(x, terms):
    parts = []
    for _ in range(terms - 1):
        p = x.astype(bf16)
        parts.append(p)
        x = x - p.astype(f32)
    parts.append(x.astype(bf16))
    return parts


def _dot01_left(m01, x, terms=3):
    return sum(jnp.dot(m01, p, preferred_element_type=f32) for p in _split(x, terms))


def _dot01_right(x, m01, terms=3):
    return sum(jnp.dot(p, m01, preferred_element_type=f32) for p in _split(x, terms))


def _params(*sem):
    return pltpu.CompilerParams(dimension_semantics=sem, vmem_limit_bytes=VMEM_LIMIT)


class _Out(NamedTuple):
    off: int
    n: int
    dtype: Any = f32
    scale: float = 1.0
    transposed: bool = False


def _in_proj_kernel(x_ref, g_ref, w_ref, *o_refs, outs):
    xn = _rms(x_ref[...], g_ref[...]).astype(bf16)
    cache = {}
    for o_ref, out in zip(o_refs, outs):
        if (out.off, out.n) not in cache:
            cache[(out.off, out.n)] = jnp.dot(xn, w_ref[:, out.off:out.off + out.n], preferred_element_type=f32)
        y = cache[(out.off, out.n)]
        if out.scale != 1.0:
            y = y * out.scale
        if out.transposed:
            o_ref[0] = y.T.astype(o_ref.dtype)
        else:
            o_ref[...] = y.astype(o_ref.dtype)


def _in_proj(x, g, w, outs, tm, seq_len=None):
    m, d = x.shape
    n_tot = w.shape[1]
    seq_len = seq_len or m
    tiles = seq_len // tm
    assert seq_len % tm == 0 and m % seq_len == 0
    return pl.pallas_call(
        functools.partial(_in_proj_kernel, outs=outs),
        grid=(m // tm,),
        in_specs=[pl.BlockSpec((tm, d), lambda i: (i, 0)),
                  pl.BlockSpec((1, d), lambda i: (0, 0)),
                  pl.BlockSpec((d, n_tot), lambda i: (0, 0))],
        out_specs=[pl.BlockSpec((1, o.n, tm), lambda i: (i // tiles, 0, i % tiles)) if o.transposed
                   else pl.BlockSpec((tm, o.n), lambda i: (i, 0)) for o in outs],
        out_shape=[jax.ShapeDtypeStruct((m // seq_len, o.n, seq_len) if o.transposed else (m, o.n), o.dtype)
                   for o in outs],
        compiler_params=_params("parallel"),
        name="in_proj",
    )(x, g, w)


POST_PROLOGUE_CHUNKS = 3


def _post_kernel(*refs, n_a, nf):
    a_refs = refs[:n_a]
    h_ref, wo_ref, g_ref, wu_ref, wd_ref, out_ref, h1_sc, xn_sc, acc_sc = refs[n_a:]
    f = pl.program_id(1)

    @pl.when(f == 0)
    def _():
        tm = h_ref.shape[0]
        step = tm // POST_PROLOGUE_CHUNKS if tm % (POST_PROLOGUE_CHUNKS * LANES) == 0 else tm
        chunks = [slice(r, r + step) for r in range(0, tm, step)]
        mix = [jnp.dot(jnp.concatenate([ref[rows, :] for ref in a_refs], axis=-1) if n_a > 1 else a_refs[0][rows, :],
                       wo_ref[...], preferred_element_type=f32) for rows in chunks]
        for rows, mx in zip(chunks, mix):
            h1 = h_ref[rows, :] + _rms(mx, g_ref[1:2, :])
            h1_sc[rows, :] = h1
            xn_sc[rows, :] = _rms(h1, g_ref[2:3, :]).astype(bf16)
        acc_sc[...] = jnp.zeros_like(acc_sc)

    u = jnp.dot(xn_sc[...], wu_ref[...], preferred_element_type=f32)
    u = jnp.maximum(u, 0.0)
    acc_sc[...] += jnp.dot((u * u).astype(bf16), wd_ref[...], preferred_element_type=f32)

    @pl.when(f == nf - 1)
    def _():
        out_ref[...] = h1_sc[...] + _rms(acc_sc[...], g_ref[3:4, :])


def _post(a_list, h, w_out, gains, w_up, w_down, tm, tf):
    m, d = h.shape
    dff = w_up.shape[1]
    nf = dff // tf
    n_a = len(a_list)
    return pl.pallas_call(
        functools.partial(_post_kernel, n_a=n_a, nf=nf),
        grid=(m // tm, nf),
        in_specs=[pl.BlockSpec((tm, a.shape[1]), lambda i, f: (i, 0)) for a in a_list] + [
            pl.BlockSpec((tm, d), lambda i, f: (i, 0)),
            pl.BlockSpec(w_out.shape, lambda i, f: (0, 0)),
            pl.BlockSpec(gains.shape, lambda i, f: (0, 0)),
            pl.BlockSpec((d, tf), lambda i, f: (0, f)),
            pl.BlockSpec((tf, d), lambda i, f: (f, 0))],
        out_specs=pl.BlockSpec((tm, d), lambda i, f: (i, 0)),
        out_shape=jax.ShapeDtypeStruct((m, d), f32),
        scratch_shapes=[pltpu.VMEM((tm, d), f32), pltpu.VMEM((tm, d), bf16), pltpu.VMEM((tm, d), f32)],
        compiler_params=_params("parallel", "arbitrary"),
        name="post",
    )(*a_list, h, w_out, gains, w_up, w_down)


MLSTM_SEQS_PER_STEP = 4

def _mlstm_kernel(q_ref, k_ref, v_ref, o_ref, g_ref, gt_ref, brow_ref, bcol_ref, c0_ref, n0_ref, m0_ref, gh_ref,
                  y_ref, C_ref, n_ref, m_ref, *, c, lo, hi):
    j = pl.program_id(1)

    @pl.when(j == 0)
    def _():
        C_ref[...] = c0_ref[...]
        n_ref[...] = n0_ref[...]
        m_ref[...] = m0_ref[...]

    row = lax.broadcasted_iota(jnp.int32, (c, c), 0)
    col = lax.broadcasted_iota(jnp.int32, (c, c), 1)
    causal = row >= col
    tril = jnp.where(causal, 1.0, 0.0).astype(bf16)
    triu = jnp.where(row <= col, 1.0, 0.0).astype(bf16)
    pos_c = j * c + lax.broadcasted_iota(jnp.int32, (c, 1), 0)
    valid_c = (pos_c >= lo) & (pos_c < hi)
    pos_r = j * c + lax.broadcasted_iota(jnp.int32, (1, c), 1)
    valid_r = (pos_r >= lo) & (pos_r < hi)

    seqs = range(q_ref.shape[0])
    gc = [g_ref[s] + brow_ref[...] for s in seqs]
    li_c = [jnp.where(valid_c, gc[s], -jnp.inf) for s in seqs]
    lf_c = [jnp.where(valid_c, _log_sigmoid(gc[s]), 0.0) for s in seqs]
    b_c = [_dot01_left(tril, lf_c[s]) for s in seqs]
    gr = [gt_ref[s] + bcol_ref[...] for s in seqs]
    li_r = [jnp.where(valid_r, gr[s], -jnp.inf) for s in seqs]
    lf_r = [jnp.where(valid_r, _log_sigmoid(gr[s]), 0.0) for s in seqs]
    b_r = [_dot01_right(lf_r[s], triu) for s in seqs]

    units = [(s, h) for s in seqs for h in range(H_A)]
    sl = {u: slice(u[1] * DK_A, (u[1] + 1) * DK_A) for u in units}
    q = {u: q_ref[u[0], :, sl[u]] for u in units}
    k = {u: k_ref[u[0], :, sl[u]] * (DK_A ** -0.5) for u in units}
    qb = {u: q[u].astype(bf16) for u in units}
    kb = {u: k[u].astype(bf16) for u in units}
    vb = {u: v_ref[u[0], :, sl[u]].astype(bf16) for u in units}
    bc = {(s, h): b_c[s][:, H_A + h:H_A + h + 1] for s, h in units}
    ic = {(s, h): li_c[s][:, h:h + 1] for s, h in units}
    br = {(s, h): b_r[s][H_A + h:H_A + h + 1, :] for s, h in units}
    ir = {(s, h): li_r[s][h:h + 1, :] for s, h in units}
    m = {(s, h): m_ref[s, h:h + 1, :] for s, h in units}
    C = {(s, h): C_ref[s, h] for s, h in units}
    n = {(s, h): n_ref[s, h:h + 1, :] for s, h in units}

    qk = {u: lax.dot_general(qb[u], kb[u], _NT, preferred_element_type=f32) for u in units}
    qc = {u: jnp.dot(qb[u], C[u].astype(bf16), preferred_element_type=f32) for u in units}
    d_log = {u: jnp.where(causal, bc[u] - br[u] + ir[u], -jnp.inf) for u in units}
    inter = {u: bc[u] + m[u] for u in units}
    m_t = {u: jnp.maximum(inter[u], jnp.max(d_log[u], axis=1, keepdims=True)) for u in units}
    sc = {u: qk[u] * jnp.exp(d_log[u] - m_t[u]) for u in units}
    sv = {u: jnp.dot(sc[u].astype(bf16), vb[u], preferred_element_type=f32) for u in units}
    w_i = {u: jnp.exp(inter[u] - m_t[u]) for u in units}
    den = {u: w_i[u] * jnp.sum(q[u] * n[u], axis=1, keepdims=True) + jnp.sum(sc[u], axis=1, keepdims=True)
           for u in units}

    b_last = {u: bc[u][c - 1:c, :] for u in units}
    gcol = {u: b_last[u] - bc[u] + ic[u] for u in units}
    m_new = {u: jnp.maximum(b_last[u] + m[u], jnp.max(gcol[u], axis=0, keepdims=True)) for u in units}
    w_d = {u: jnp.exp(b_last[u] + m[u] - m_new[u]) for u in units}
    kw = {u: k[u] * jnp.exp(gcol[u] - m_new[u]) for u in units}
    kv = {u: lax.dot_general(kw[u].astype(bf16), vb[u], _TN, preferred_element_type=f32) for u in units}
    for s, h in units:
        u = (s, h)
        C_ref[s, h] = w_d[u] * C[u] + kv[u]
        n_ref[s, h:h + 1, :] = w_d[u] * n[u] + jnp.sum(kw[u], axis=0, keepdims=True)
        m_ref[s, h:h + 1, :] = m_new[u]

    for s, h in units:
        u = (s, h)
        hh = (w_i[u] * qc[u] + sv[u]) / jnp.maximum(jnp.abs(den[u]), jnp.exp(-m_t[u]))
        y = hh * lax.rsqrt(jnp.mean(hh * hh, axis=1, keepdims=True) + EPS) * gh_ref[:, sl[u]]
        y_ref[s, :, sl[u]] = (y * jax.nn.sigmoid(o_ref[s, :, sl[u]])).astype(y_ref.dtype)


def _mlstm(q, k, v, o, g, gt, brow, bcol, c0, n0, m0, gh, lo, hi):
    nb, length, _ = q.shape
    c = CHUNK_A
    ns = MLSTM_SEQS_PER_STEP if nb % MLSTM_SEQS_PER_STEP == 0 else 1
    wide = pl.BlockSpec((ns, c, H_A * DK_A), lambda b, j: (b, j, 0))
    st = lambda shape: pl.BlockSpec((ns,) + shape, lambda b, j: (b,) + (0,) * len(shape))
    const = lambda a: pl.BlockSpec(a.shape, lambda b, j: (0,) * a.ndim)
    return pl.pallas_call(
        functools.partial(_mlstm_kernel, c=c, lo=lo, hi=hi),
        grid=(nb // ns, length // c),
        in_specs=[wide, wide, wide, wide,
                  pl.BlockSpec((ns, c, LANES), lambda b, j: (b, j, 0)),
                  pl.BlockSpec((ns, 2 * H_A, c), lambda b, j: (b, 0, j)),
                  const(brow), const(bcol),
                  st((H_A, DK_A, DV_A)), st((H_A, DK_A)), st((H_A, 1)), const(gh)],
        out_specs=[wide, st((H_A, DK_A, DV_A)), st((H_A, DK_A)), st((H_A, 1))],
        out_shape=[jax.ShapeDtypeStruct((nb, length, H_A * DV_A), bf16),
                   jax.ShapeDtypeStruct((nb, H_A, DK_A, DV_A), f32),
                   jax.ShapeDtypeStruct((nb, H_A, DK_A), f32),
                   jax.ShapeDtypeStruct((nb, H_A, 1), f32)],
        compiler_params=_params("parallel", "arbitrary"),
        name="mlstm",
    )(q, k, v, o, g, gt, brow, bcol, c0, n0, m0, gh)


SB_HEADS = 4


def _neg_cumsum_table(n, inclusive=False):
    j = np.arange(n)[:, None]
    s = np.arange(n)[None, :]
    later = (j >= s) if inclusive else (j > s)
    half = np.concatenate([later.astype(np.float32), np.ones((n, n), np.float32)], axis=1)
    return jnp.asarray(-np.concatenate([half, half], axis=0), bf16)


def _sb_prompt_kernel(bias_ref, q_ref, k_ref, v_ref, tab_ref, o_ref, carry_ref, acc_ref, *, tq, lo):
    hg = pl.program_id(1)
    i = pl.program_id(2)
    sub = tab_ref.shape[0] // 2
    lw = SB_HEADS * DH_B
    lane = lax.broadcasted_iota(jnp.int32, (tq, lw), 1)
    qg = q_ref[0]
    qm = [jnp.where(lane // DH_B == e, qg, jnp.zeros_like(qg)) for e in range(SB_HEADS)]
    bias = [bias_ref[SB_HEADS * hg + e] for e in range(SB_HEADS)]
    heads = range(SB_HEADS)
    carry_ref[...] = jnp.zeros_like(carry_ref)
    acc_ref[...] = jnp.zeros_like(acc_ref)

    def keys(ref, tau):
        return ref[0, pl.ds(pl.multiple_of(tau * tq, tq), tq), :]

    nsub = tq // sub

    def sweep(taus, key_bias):
        nt = len(taus)
        units = [(a, e) for a in range(nt) for e in heads]
        s0 = pl.multiple_of(taus[-1] * tq, tq)
        kt = k_ref[0, pl.ds(s0, nt * tq), :]
        vt = v_ref[0, pl.ds(s0, nt * tq), :]
        zz = [lax.dot_general(qm[e], kt, _NT, preferred_element_type=f32) for e in heads]
        z = {(a, e): zz[e][:, (nt - 1 - a) * tq:(nt - a) * tq] + key_bias[a][e] for a, e in units}
        sp = {u: _softplus(z[u]) for u in units}
        hi = {u: sp[u].astype(bf16) for u in units}
        lo_ = {u: (sp[u] - hi[u].astype(f32)).astype(bf16) for u in units}
        r = {u: [jnp.dot(jnp.concatenate([hi[u][:, j * sub:(j + 1) * sub], lo_[u][:, j * sub:(j + 1) * sub]], axis=1),
                         tab_ref[...], preferred_element_type=f32) for j in range(nsub)] for u in units}
        t = {}
        for e in heads:
            c = carry_ref[e]
            for a in range(len(taus)):
                after = [None] * nsub
                for j in reversed(range(nsub)):
                    after[j] = r[a, e][j][:, :sub] + c
                    c = c + r[a, e][j][:, sub:]
                t[a, e] = z[a, e] + jnp.concatenate(after, axis=1)
            carry_ref[e] = c
        w = {u: jnp.exp(t[u]).astype(bf16) for u in units}
        for e in heads:
            we = jnp.concatenate([w[a, e] for a in reversed(range(nt))], axis=1) if nt > 1 else w[0, e]
            acc_ref[e] += jnp.dot(we, vt, preferred_element_type=f32)

    def row_bias(tau):
        spos = tau * tq + lax.broadcasted_iota(jnp.int32, (1, tq), 1)
        return [jnp.where(spos >= lo, b, -jnp.inf) for b in bias]

    row = lax.broadcasted_iota(jnp.int32, (tq, tq), 0)
    col = lax.broadcasted_iota(jnp.int32, (tq, tq), 1)
    readable = (col < row) & (i * tq + col >= lo)
    sweep([i], [[jnp.where(readable, b, -jnp.inf) for b in bias]])

    @pl.when(i % 2 == 1)
    def _():
        sweep([i - 1], [row_bias(i - 1)])

    def body(n, carry):
        tau = i - 1 - i % 2 - 2 * n
        sweep([tau, tau - 1], [row_bias(tau), row_bias(tau - 1)])
        return carry

    lax.fori_loop(0, i // 2, body, 0)
    out = acc_ref[0]
    for e in range(1, SB_HEADS):
        out = jnp.where(lane // DH_B == e, acc_ref[e], out)
    o_ref[0] = out.astype(o_ref.dtype)


def _sb_prompt(q, k, v, bias, lo):
    nb, length, width = q.shape
    nblk = length // BLOCK
    tq = BLOCK * max(d for d in (1, 2, 3, 4) if nblk % d == 0)
    lw = SB_HEADS * DH_B
    blk = pl.BlockSpec((1, tq, lw), lambda b, hg, i: (b, i, hg))
    full = pl.BlockSpec((1, length, lw), lambda b, hg, i: (b, 0, hg))
    tab = _neg_cumsum_table(BLOCK, inclusive=True)
    return pl.pallas_call(
        functools.partial(_sb_prompt_kernel, tq=tq, lo=lo),
        grid=(nb, width // lw, length // tq),
        in_specs=[pl.BlockSpec(memory_space=pltpu.SMEM), blk, full, full,
                  pl.BlockSpec(tab.shape, lambda b, hg, i: (0, 0))],
        out_specs=blk,
        out_shape=jax.ShapeDtypeStruct((nb, length, width), bf16),
        scratch_shapes=[pltpu.VMEM((SB_HEADS, tq, BLOCK), f32), pltpu.VMEM((SB_HEADS, tq, lw), f32)],
        compiler_params=_params("parallel", "parallel", "arbitrary"),
        name="sb_prompt",
    )(bias, q, k, v, tab)


QH = 32
SB_PAGES_PER_STEP = 32


def _sb_sample_kernel(pt_ref, q_ref, kn_ref, vn_ref, bias_ref, tab_ref, *rest, n_pg, n_grp, n_tok):
    k_refs = rest[:n_pg]
    v_refs = rest[n_pg:2 * n_pg]
    o_ref, carry_sc, acc_sc = rest[2 * n_pg:]
    g = pl.program_id(1)
    width = H_B * DH_B
    psz = tab_ref.shape[0] // 2
    row_head = lax.broadcasted_iota(jnp.int32, (QH, width), 0) % H_B
    lane_head = lax.broadcasted_iota(jnp.int32, (QH, width), 1) // DH_B
    q = q_ref[0]
    qrows = jnp.concatenate([jnp.broadcast_to(q[t:t + 1, :], (H_B, width)) for t in range(n_tok)], axis=0)
    qcat = jnp.where(lane_head == row_head, qrows, jnp.zeros_like(qrows)).astype(bf16)

    def weights(zs):
        n = len(zs)
        z = jnp.concatenate(zs, axis=0) if n > 1 else zs[0]
        sp = _softplus(z)
        hi = sp.astype(bf16)
        lo_ = (sp - hi.astype(f32)).astype(bf16)
        r = jnp.dot(jnp.concatenate([hi, lo_], axis=1), tab_ref[...], preferred_element_type=f32)
        c = carry_sc[...]
        after = [None] * n
        for p in reversed(range(n)):
            after[p] = r[p * QH:(p + 1) * QH, :psz] + c
            c = c + r[p * QH:(p + 1) * QH, psz:]
        carry_sc[...] = c
        w = jnp.exp(z - sp + (jnp.concatenate(after, axis=0) if n > 1 else after[0])).astype(bf16)
        return [w[p * QH:(p + 1) * QH, :] for p in range(n)]

    @pl.when(g == 0)
    def _():
        carry_sc[...] = jnp.zeros_like(carry_sc)
        pad = jnp.zeros((psz - kn_ref.shape[1], width), f32)
        kn = jnp.concatenate([kn_ref[0], pad], axis=0).astype(bf16)
        vn = jnp.concatenate([vn_ref[0], pad], axis=0).astype(bf16)
        j = lax.broadcasted_iota(jnp.int32, (QH, psz), 1)
        t = lax.broadcasted_iota(jnp.int32, (QH, psz), 0) // H_B
        z = lax.dot_general(qcat, kn, _NT, preferred_element_type=f32) + jnp.where(j < t, bias_ref[...], -jnp.inf)
        acc_sc[...] = jnp.dot(weights([z])[0], vn, preferred_element_type=f32)

    ws = weights([jnp.dot(qcat, k_refs[p][0, 0].astype(bf16), preferred_element_type=f32) + bias_ref[...]
                  for p in range(n_pg)])
    acc = acc_sc[...]
    for p in range(n_pg):
        acc = acc + lax.dot_general(ws[p], v_refs[p][0, 0].astype(bf16), _NT, preferred_element_type=f32)
    acc_sc[...] = acc

    @pl.when(g == n_grp - 1)
    def _():
        own = jnp.where(lane_head == row_head, acc_sc[...], 0.0)
        rows = [jnp.sum(own[t * H_B:(t + 1) * H_B, :], axis=0, keepdims=True) for t in range(n_tok)]
        rows.append(jnp.zeros((o_ref.shape[1] - n_tok, width), f32))
        o_ref[0] = jnp.concatenate(rows, axis=0).astype(o_ref.dtype)


def _sb_sample(q, kn, vn, bias, cache_k, cache_v, page_table, layer, n_tok):
    db, rows, width = q.shape
    n_pages = page_table.shape[1]
    n_layers, n_pool, psz = cache_k.shape[:3]
    n_pg = next(d for d in range(min(SB_PAGES_PER_STEP, n_pages), 0, -1) if n_pages % d == 0)
    n_grp = n_pages // n_pg
    assert n_tok * H_B == QH
    tab = _neg_cumsum_table(psz)
    cache_k = jnp.transpose(cache_k, (0, 1, 3, 4, 2)).reshape(n_layers, n_pool, width, psz)
    cache_v = jnp.transpose(cache_v, (0, 1, 3, 4, 2)).reshape(n_layers, n_pool, width, psz)

    def page_spec(p):
        return pl.BlockSpec((1, 1, width, psz),
                            lambda b, g, pt: (layer, pt[b, (n_grp - 1 - g) * n_pg + p], 0, 0))

    small = pl.BlockSpec((1, rows, width), lambda b, g, pt: (b, 0, 0))
    grid_spec = pltpu.PrefetchScalarGridSpec(
        num_scalar_prefetch=1,
        grid=(db, n_grp),
        in_specs=[small, small, small,
                  pl.BlockSpec(bias.shape, lambda b, g, pt: (0, 0)),
                  pl.BlockSpec(tab.shape, lambda b, g, pt: (0, 0))]
        + [page_spec(p) for p in range(n_pg)] * 2,
        out_specs=small,
        scratch_shapes=[pltpu.VMEM((QH, psz), f32), pltpu.VMEM((QH, width), f32)])
    return pl.pallas_call(
        functools.partial(_sb_sample_kernel, n_pg=n_pg, n_grp=n_grp, n_tok=n_tok),
        grid_spec=grid_spec,
        out_shape=jax.ShapeDtypeStruct((db, rows, width), bf16),
        compiler_params=_params("parallel", "arbitrary"),
        name="sb_sample",
    )(page_table, q, kn, vn, bias, tab, *([cache_k] * n_pg), *([cache_v] * n_pg))


GLA_CHUNKS_PER_STEP = 2
GLA_SEQS_PER_STEP = 4


def _gla_tables(c):
    t = np.arange(c)[:, None]
    j = np.arange(c)[None, :]
    mats = [(j <= t), (j > t)]
    s = 1
    while s < c:
        bound = (t // (2 * s)) * (2 * s) + s - 1
        right = (t // s) % 2 == 1
        mats.append(np.where(right, (j > bound) & (j <= t), (j > t) & (j <= bound)))
        s *= 2
    return jnp.asarray(np.concatenate(mats, axis=0).astype(np.float32), bf16)


def _gla_kernel(q_ref, k_ref, v_ref, r_ref, gd_ref, wg_ref, bg_ref, e_ref, s0_ref, gh_ref,
                y_ref, S_ref, *, c, nck, lo, hi):
    j = pl.program_id(1)

    @pl.when(j == 0)
    def _():
        S_ref[...] = s0_ref[...]


    rows = nck * c
    pos = j * rows + lax.broadcasted_iota(jnp.int32, (rows, 1), 0)
    valid = (pos >= lo) & (pos < hi)
    rowc = lax.broadcasted_iota(jnp.int32, (c, 1), 0)
    row = lax.broadcasted_iota(jnp.int32, (c, c), 0)
    col = lax.broadcasted_iota(jnp.int32, (c, c), 1)
    seqs = range(q_ref.shape[0])
    lg = [jnp.where(valid, _log_sigmoid(jnp.dot(gd_ref[s].astype(bf16), wg_ref[...], preferred_element_type=f32)
                                        + bg_ref[...]) / GATE_TAU, 0.0) for s in seqs]
    n_lvl = e_ref.shape[0] // c - 2

    heads = range(H_C)
    items = [(s, h, u) for s in seqs for h in heads for u in range(nck)]
    sk = {(s, h, u): (s, slice(u * c, (u + 1) * c), slice(h * DK_C, (h + 1) * DK_C)) for s, h, u in items}
    sv = {(s, h, u): (s, slice(u * c, (u + 1) * c), slice(h * DV_C, (h + 1) * DV_C)) for s, h, u in items}
    q = {i: q_ref[sk[i]] * (DK_C ** -0.5) for i in items}
    k = {i: jnp.where(valid[sk[i][1], :], k_ref[sk[i]], 0.0) for i in items}
    vb = {i: v_ref[sv[i]].astype(bf16) for i in items}
    x = {i: _dot01_left(e_ref[...], lg[i[0]][sk[i][1], sk[i][2]], terms=2) for i in items}
    fx = {i: jnp.exp(x[i]) for i in items}
    kv = {i: lax.dot_general((k[i] * fx[i][c:2 * c, :]).astype(bf16), vb[i], _TN, preferred_element_type=f32)
          for i in items}
    a = {i: jnp.where(row == col, jnp.sum(q[i] * k[i], axis=1, keepdims=True), 0.0) for i in items}
    for lvl in range(n_lvl):
        s = 1 << lvl
        right = (rowc // s) % 2 == 1
        same = row // (2 * s) == col // (2 * s)
        fl = {i: fx[i][(2 + lvl) * c:(3 + lvl) * c, :] for i in items}
        qt = {i: jnp.where(right, q[i] * fl[i], 0.0).astype(bf16) for i in items}
        kt = {i: jnp.where(right, 0.0, k[i] * fl[i]).astype(bf16) for i in items}
        p = {i: lax.dot_general(qt[i], kt[i], _NT, preferred_element_type=f32) for i in items}
        a = {i: a[i] + jnp.where(same, p[i], 0.0) for i in items}
    o = {i: jnp.dot(a[i].astype(bf16), vb[i], preferred_element_type=f32) for i in items}
    for s in seqs:
        for h in heads:
            S = S_ref[s, h]
            for u in range(nck):
                i = (s, h, u)
                o[i] = o[i] + jnp.dot((q[i] * fx[i][0:c, :]).astype(bf16), S.astype(bf16),
                                      preferred_element_type=f32)
                b_last = x[i][c - 1:c, :]
                decay = jnp.exp(jnp.transpose(jnp.broadcast_to(b_last, (DK_C, DK_C))))
                S = jnp.concatenate([decay] * (DV_C // DK_C), axis=1) * S + kv[i]
            S_ref[s, h] = S
    for i in items:
        y = o[i] * lax.rsqrt(jnp.mean(o[i] * o[i], axis=1, keepdims=True) + EPS) * gh_ref[:, sv[i][2]]
        y_ref[sv[i]] = (y * jax.nn.silu(r_ref[sv[i]])).astype(y_ref.dtype)


def _gla(q, k, v, r, gd, wg, bg, s0, gh, lo, hi):
    nb, length, _ = q.shape
    c = CHUNK_C
    nck = min(GLA_CHUNKS_PER_STEP, length // c)
    ns = GLA_SEQS_PER_STEP * GLA_CHUNKS_PER_STEP // nck
    ns = ns if nb % ns == 0 else 1
    e = _gla_tables(c)
    blk = lambda w: pl.BlockSpec((ns, nck * c, w), lambda b, j: (b, j, 0))
    const = lambda a: pl.BlockSpec(a.shape, lambda b, j: (0,) * a.ndim)
    st = pl.BlockSpec((ns, H_C, DK_C, DV_C), lambda b, j: (b, 0, 0, 0))
    return pl.pallas_call(
        functools.partial(_gla_kernel, c=c, nck=nck, lo=lo, hi=hi),
        grid=(nb // ns, length // (nck * c)),
        in_specs=[blk(H_C * DK_C), blk(H_C * DK_C), blk(H_C * DV_C), blk(H_C * DV_C), blk(LANES),
                  const(wg), const(bg), const(e), st, const(gh)],
        out_specs=[blk(H_C * DV_C), st],
        out_shape=[jax.ShapeDtypeStruct((nb, length, H_C * DV_C), bf16),
                   jax.ShapeDtypeStruct((nb, H_C, DK_C, DV_C), f32)],
        compiler_params=_params("parallel", "arbitrary"),
        name="gla",
    )(q, k, v, r, gd, wg, bg, e, s0, gh)


def _pad_rows(x, rows):
    return jnp.pad(x, ((0, 0), (0, rows - x.shape[1]), (0, 0)))


POST_ROW_TILES = (768, 512, 256, 128)
POST_FF_TILE = 2048


def _row_tile(m, candidates):
    for tm in candidates:
        if m % tm == 0:
            return tm
    raise ValueError(f"row count {m} is not a multiple of {candidates[-1]}")


def kernel(x_prompt, x_sample, cache_sb_k, cache_sb_v, state_mlstm_c, state_mlstm_n, state_mlstm_m, state_gla_s,
           page_table, meta_tokens, norm_gains, w_in_even, b_gate_even, b_sb_even, g_head_even, w_out_even,
           w_in_odd, w_gate_up_odd, b_gate_up_odd, g_head_odd, w_out_odd, w_up, w_down):
    B, seq, D = x_prompt.shape
    DB, T, _ = x_sample.shape
    depth = norm_gains.shape[0]
    n_meta = meta_tokens.shape[0]
    meta_pad = BLOCK - n_meta
    Lp = meta_pad + n_meta + seq
    assert Lp % BLOCK == 0 and (DB * T) % LANES == 0 and T <= SUBLANES

    hp = jnp.concatenate([jnp.zeros((B, meta_pad, D), f32),
                          jnp.broadcast_to(meta_tokens[None], (B, n_meta, D)), x_prompt], axis=1)
    hp = hp.reshape(B * Lp, D)
    hs = x_sample.reshape(DB * T, D)
    tm_p, tm_s = _row_tile(B * Lp, POST_ROW_TILES), _row_tile(DB * T, POST_ROW_TILES)
    tm_in = _row_tile(Lp, (384, 256, 128))

    na, nb_ = H_A * DK_A, H_B * DH_B
    nk, nv = H_C * DK_C, H_C * DV_C
    o_qa, o_ka, o_va, o_oa = 0, na, 2 * na, 3 * na
    o_qb, o_kb, o_vb, o_g = 4 * na, 4 * na + nb_, 4 * na + 2 * nb_, 4 * na + 3 * nb_
    even_common = (_Out(o_qa, na), _Out(o_ka, na), _Out(o_va, na), _Out(o_oa, na), _Out(o_g, LANES),
                   _Out(o_qb, nb_, bf16, DH_B ** -0.5), _Out(o_kb, nb_, bf16), _Out(o_vb, nb_, bf16))
    even_outs_p = even_common + (_Out(o_kb, nb_, transposed=True), _Out(o_vb, nb_, transposed=True))
    even_outs_s = even_common + (_Out(o_kb, nb_), _Out(o_vb, nb_))
    odd_outs = (_Out(0, nk), _Out(nk, nk), _Out(2 * nk, nv), _Out(2 * nk + nv, nv), _Out(2 * nk + 2 * nv, LANES))

    outs = {n: [] for n in ("pk", "pv", "pc", "pn", "pm", "ps", "sk", "sv", "sc", "sn", "sm", "ss")}

    for layer in range(depth):
        gains = norm_gains[layer]
        if layer % 2 == 0:
            e = layer // 2
            w = w_in_even[e]
            w = jnp.concatenate([w[:, :4 * na], w[:, 4 * na + 2 * H_A:], w[:, 4 * na:4 * na + 2 * H_A],
                                 jnp.zeros((D, LANES - 2 * H_A), f32)], axis=1).astype(bf16)
            brow = jnp.pad(b_gate_even[e], (0, LANES - 2 * H_A)).reshape(1, LANES)
            bcol = b_gate_even[e].reshape(2 * H_A, 1)
            gh = g_head_even[e].reshape(1, -1)
            w_out = w_out_even[e].astype(bf16)

            qa, ka, va, oa, gt, qb16, kb16, vb16, kbt, vbt = _in_proj(hp, gains[0:1], w, even_outs_p, tm_in, Lp)
            r3 = lambda a: a.reshape(B, Lp, a.shape[-1])
            g3 = r3(gt)
            ya, C, n, m = _mlstm(r3(qa), r3(ka), r3(va), r3(oa), g3, jnp.swapaxes(g3[:, :, :2 * H_A], 1, 2),
                                 brow, bcol, jnp.zeros((B, H_A, DK_A, DV_A), f32), jnp.zeros((B, H_A, DK_A), f32),
                                 jnp.zeros((B, H_A, 1), f32), gh, meta_pad, Lp)
            hb = _sb_prompt(r3(qb16), r3(kb16), r3(vb16), b_sb_even[e], meta_pad)
            hp = _post([ya.reshape(B * Lp, -1), hb.reshape(B * Lp, -1)], hp, w_out, gains,
                       w_up[layer].astype(bf16), w_down[layer].astype(bf16), tm_p, POST_FF_TILE)
            rows = lambda t: jnp.transpose(t.reshape(B, H_B, DH_B, Lp)[..., meta_pad:], (0, 3, 1, 2))
            outs["pk"].append(rows(kbt))
            outs["pv"].append(rows(vbt))
            outs["pc"].append(C)
            outs["pn"].append(n)
            outs["pm"].append(m.reshape(B, H_A))

            qa, ka, va, oa, gt, qb16, kb16, vb16, kb, vb = _in_proj(hs, gains[0:1], w, even_outs_s, tm_s)
            s3 = lambda a: a.reshape(DB, T, a.shape[-1])
            pc = lambda a: _pad_rows(s3(a), CHUNK_A)
            g3 = pc(gt)
            ya, C, n, m = _mlstm(pc(qa), pc(ka), pc(va), pc(oa), g3, jnp.swapaxes(g3[:, :, :2 * H_A], 1, 2),
                                 brow, bcol, state_mlstm_c[e], state_mlstm_n[e],
                                 state_mlstm_m[e].reshape(DB, H_A, 1), gh, 0, T)
            p8 = lambda a: _pad_rows(s3(a), SUBLANES)
            bias_rows = jnp.broadcast_to(jnp.tile(b_sb_even[e], T)[:, None], (T * H_B, cache_sb_k.shape[2]))
            hb = _sb_sample(p8(qb16.astype(f32)), p8(kb), p8(vb), bias_rows, cache_sb_k, cache_sb_v, page_table, e, T)
            hs = _post([ya[:, :T].reshape(DB * T, -1), hb[:, :T].reshape(DB * T, -1)], hs, w_out, gains,
                       w_up[layer].astype(bf16), w_down[layer].astype(bf16), tm_s, POST_FF_TILE)
            outs["sk"].append(kb.reshape(DB, T, H_B, DH_B))
            outs["sv"].append(vb.reshape(DB, T, H_B, DH_B))
            outs["sc"].append(C)
            outs["sn"].append(n)
            outs["sm"].append(m.reshape(DB, H_A))
        else:
            o = layer // 2
            w = jnp.concatenate([w_in_odd[o], jnp.zeros((D, LANES - GATE_RANK), f32)], axis=1).astype(bf16)
            wg = jnp.concatenate([w_gate_up_odd[o], jnp.zeros((LANES - GATE_RANK, nk), f32)], axis=0).astype(bf16)
            bg = b_gate_up_odd[o].reshape(1, nk)
            gh = g_head_odd[o].reshape(1, -1)
            w_out = w_out_odd[o].astype(bf16)

            q, k, v, r, gd = _in_proj(hp, gains[0:1], w, odd_outs, tm_in)
            r3 = lambda a: a.reshape(B, Lp, a.shape[-1])
            y, S = _gla(r3(q), r3(k), r3(v), r3(r), r3(gd), wg, bg, jnp.zeros((B, H_C, DK_C, DV_C), f32), gh,
                        meta_pad, Lp)
            hp = _post([y.reshape(B * Lp, -1)], hp, w_out, gains,
                       w_up[layer].astype(bf16), w_down[layer].astype(bf16), tm_p, POST_FF_TILE)
            outs["ps"].append(S)

            q, k, v, r, gd = _in_proj(hs, gains[0:1], w, odd_outs, tm_s)
            pc = lambda a: _pad_rows(a.reshape(DB, T, a.shape[-1]), CHUNK_C)
            y, S = _gla(pc(q), pc(k), pc(v), pc(r), pc(gd), wg, bg, state_gla_s[o], gh, 0, T)
            hs = _post([y[:, :T].reshape(DB * T, -1)], hs, w_out, gains,
                       w_up[layer].astype(bf16), w_down[layer].astype(bf16), tm_s, POST_FF_TILE)
            outs["ss"].append(S)

    y_prompt = hp.reshape(B, Lp, D)[:, meta_pad + n_meta:]
    y_sample = hs.reshape(DB, T, D)
    st = jnp.stack
    return (y_prompt, y_sample, st(outs["pk"]), st(outs["pv"]), st(outs["pc"]), st(outs["pn"]), st(outs["pm"]),
            st(outs["ps"]), st(outs["sk"]), st(outs["sv"]), st(outs["sc"]), st(outs["sn"]), st(outs["sm"]),
            st(outs["ss"]))
```
